```python
import math
import jax, jax.numpy as jnp
from jax import lax
import numpy as np

D_MODEL = 1024
BATCH = 8
SEQ = 8192
DEPTH = 1

HEAD_DIM = 64
NA_HEADS = 8
NA_WIDTH = NA_HEADS * HEAD_DIM
NA_KR_MAX = 8
NA_KC = 16
GRID_W = 64
WG_HEADS = 8
WG_KV_HEADS = 2
WG_WIDTH = WG_HEADS * HEAD_DIM
WG_KV_WIDTH = WG_KV_HEADS * HEAD_DIM
WINDOW = 128
WG_BLOCK = 128
N_EXPERTS = 16
EC_CAPACITY_FACTOR = 2
D_FF = 2048
IN_SPLITS = (NA_WIDTH, NA_WIDTH, NA_WIDTH, WG_WIDTH, WG_KV_WIDTH, WG_KV_WIDTH, D_MODEL, D_MODEL)
IN_WIDTH = sum(IN_SPLITS)
LN_EPS = 1e-5
DN_ALPHA = (2 * DEPTH) ** 0.25
DN_BETA = (8 * DEPTH) ** -0.25
NEG_INF = -1e30

kernel_name = "hybrid_natten_wgqa_ecmoe_deepnorm"


def layer_norm(x, g, b):
    xf = x.astype(jnp.float32)
    mu = jnp.mean(xf, -1, keepdims=True)
    var = jnp.mean(jnp.square(xf - mu), -1, keepdims=True)
    return ((xf - mu) * lax.rsqrt(var + LN_EPS) * g + b).astype(x.dtype)


def alibi_slopes(n_heads):
    return jnp.exp2(-8.0 * (jnp.arange(n_heads, dtype=jnp.float32) + 1.0) / n_heads)


def neighbourhood_attention(q, k, v, rpb):
    B, S, H, Dh = q.shape
    rows = S // GRID_W
    kr = min(NA_KR_MAX, rows)
    qg = (q * (Dh ** -0.5)).reshape(B, rows, GRID_W, H, Dh)
    kg = k.reshape(B, rows, GRID_W, H, Dh)
    vg = v.reshape(B, rows, GRID_W, H, Dh)
    cols = jnp.arange(GRID_W)
    col_start = jnp.clip(cols - NA_KC // 2, 0, GRID_W - NA_KC)
    col_idx = col_start[:, None] + jnp.arange(NA_KC)[None, :]
    col_off = col_idx - cols[:, None] + (NA_KC - 1)

    def row_block(r):
        r0 = jnp.clip(r - kr // 2, 0, rows - kr)
        q_r = lax.dynamic_index_in_dim(qg, r, axis=1, keepdims=False)
        k_r = lax.dynamic_slice_in_dim(kg, r0, kr, axis=1)
        v_r = lax.dynamic_slice_in_dim(vg, r0, kr, axis=1)
        k_win = k_r[:, :, col_idx]
        v_win = v_r[:, :, col_idx]
        row_off = r0 + jnp.arange(kr) - r + (NA_KR_MAX - 1)
        bias = rpb[:, row_off[:, None, None], col_off[None, :, :]]
        s = jnp.einsum('bchd,bicjhd->bhcij', q_r, k_win).astype(jnp.float32)
        s = s + bias.transpose(0, 2, 1, 3)[None].astype(jnp.float32)
        p = jax.nn.softmax(s.reshape(B, H, GRID_W, kr * NA_KC), axis=-1)
        p = p.reshape(B, H, GRID_W, kr, NA_KC).astype(v.dtype)
        return jnp.einsum('bhcij,bicjhd->bchd', p, v_win)

    out = lax.map(row_block, jnp.arange(rows))
    return out.transpose(1, 0, 2, 3, 4).reshape(B, S, H * Dh)


def windowed_gqa_sample(q, k, v, sink):
    S, HQ, Dh = q.shape
    HKV = k.shape[1]
    G = HQ // HKV
    L = WG_BLOCK
    nb = S // L
    qb = (q * (Dh ** -0.5)).reshape(nb, L, HKV, G, Dh)

    def band(t):
        tb = jnp.pad(t.reshape(nb, L, HKV, Dh), ((1, 1), (0, 0), (0, 0), (0, 0)))
        return jnp.concatenate([tb[:-2], tb[1:-1], tb[2:]], axis=1)

    kb, vb = band(k), band(v)
    s = jnp.einsum('nqkgd,nskd->nkgqs', qb, kb).astype(jnp.float32)
    q_loc = jnp.arange(L)[:, None]
    s_loc = jnp.arange(3 * L)[None, :] - L
    dist = jnp.abs(s_loc - q_loc).astype(jnp.float32)
    key_pos = jnp.arange(nb)[:, None] * L + s_loc
    valid = (dist <= WINDOW)[None] & ((key_pos >= 0) & (key_pos < S))[:, None, :]
    slopes = alibi_slopes(HQ).reshape(HKV, G)
    s = s - slopes[None, :, :, None, None] * dist[None, None, None]
    s = jnp.where(valid[:, None, None], s, NEG_INF)
    sink_l = sink.astype(jnp.float32).reshape(HKV, G)[None, :, :, None, None]
    m = jnp.maximum(jnp.max(s, -1, keepdims=True), sink_l)
    e = jnp.exp(s - m)
    p = e / (jnp.sum(e, -1, keepdims=True) + jnp.exp(sink_l - m))
    o = jnp.einsum('nkgqs,nskd->nqkgd', p.astype(v.dtype), vb)
    return o.reshape(S, HQ * Dh)


def expert_choice_moe(x, w_router, w_gate, w_up, w_down):
    B, n, D = x.shape
    E = w_router.shape[1]
    cap = EC_CAPACITY_FACTOR * n // E
    aff = jax.nn.softmax(jnp.einsum('bnd,de->ben', x, w_router).astype(jnp.float32), axis=1)
    gates, idx = lax.top_k(aff, cap)
    idx_e = idx.transpose(1, 0, 2)
    xin = x[jnp.arange(B)[None, :, None], idx_e]

    def expert(args):
        xe, wg, wu, wd = args
        h = jax.nn.silu(xe @ wg) * (xe @ wu)
        return h @ wd

    out = lax.map(expert, (xin, w_gate, w_up, w_down))
    out = out * gates.transpose(1, 0, 2)[..., None].astype(out.dtype)

    def combine(ib, ob):
        return jnp.zeros((n, D), ob.dtype).at[ib.reshape(-1)].add(ob.reshape(-1, D))

    return jax.vmap(combine, in_axes=(1, 1))(idx_e, out)


def setup_inputs(seed: int = 0) -> dict:
    key = jax.random.key(seed)
    ks = jax.random.split(key, 17)
    f = jnp.float32
    nrm = lambda k, shape, s: jax.random.normal(k, shape, f) * s
    L = DEPTH
    return {
        "x": nrm(ks[0], (BATCH, SEQ, D_MODEL), 1.0),
        "w_in": nrm(ks[1], (L, D_MODEL, IN_WIDTH), D_MODEL ** -0.5),
        "b_in": nrm(ks[2], (L, IN_WIDTH), 0.01),
        "rpb": nrm(ks[3], (L, NA_HEADS, 2 * NA_KR_MAX - 1, 2 * NA_KC - 1), 0.02),
        "sink": nrm(ks[4], (L, WG_HEADS), 0.5),
        "w_branch_a": nrm(ks[5], (L, NA_WIDTH, D_MODEL), DN_BETA * NA_WIDTH ** -0.5),
        "w_branch_b": nrm(ks[6], (L, WG_WIDTH, D_MODEL), DN_BETA * WG_WIDTH ** -0.5),
        "w_out": nrm(ks[7], (L, D_MODEL, D_MODEL), DN_BETA * D_MODEL ** -0.5),
        "ln1_g": 1.0 + nrm(ks[8], (L, D_MODEL), 0.02),
        "ln1_b": nrm(ks[9], (L, D_MODEL), 0.01),
        "w_router": nrm(ks[10], (L, D_MODEL, N_EXPERTS), D_MODEL ** -0.5),
        "w_gate": nrm(ks[11], (L, N_EXPERTS, D_MODEL, D_FF), D_MODEL ** -0.5),
        "w_up": nrm(ks[12], (L, N_EXPERTS, D_MODEL, D_FF), D_MODEL ** -0.5),
        "w_down": nrm(ks[13], (L, N_EXPERTS, D_FF, D_MODEL), DN_BETA * D_FF ** -0.5),
        "ln2_g": 1.0 + nrm(ks[14], (L, D_MODEL), 0.02),
        "ln2_b": nrm(ks[15], (L, D_MODEL), 0.01),
    }


def reference(x, w_in, b_in, rpb, sink, w_branch_a, w_branch_b, w_out, ln1_g, ln1_b,
              w_router, w_gate, w_up, w_down, ln2_g, ln2_b):
    B, S, D = x.shape
    offsets = list(np.cumsum(IN_SPLITS)[:-1])
    for l in range(DEPTH):
        proj = jnp.einsum('bsd,de->bse', x, w_in[l]) + b_in[l]
        qa, ka, va, qb, kb, vb, ga, gb = jnp.split(proj, offsets, axis=-1)
        ya = neighbourhood_attention(qa.reshape(B, S, NA_HEADS, HEAD_DIM),
                                     ka.reshape(B, S, NA_HEADS, HEAD_DIM),
                                     va.reshape(B, S, NA_HEADS, HEAD_DIM), rpb[l])
        sink_l = sink[l]
        yb = lax.map(lambda a: windowed_gqa_sample(a[0], a[1], a[2], sink_l),
                     (qb.reshape(B, S, WG_HEADS, HEAD_DIM),
                      kb.reshape(B, S, WG_KV_HEADS, HEAD_DIM),
                      vb.reshape(B, S, WG_KV_HEADS, HEAD_DIM)))
        mix = (jax.nn.sigmoid(ga) * (ya @ w_branch_a[l])
               + jax.nn.sigmoid(gb) * (yb @ w_branch_b[l]))
        x = layer_norm(DN_ALPHA * x + mix @ w_out[l], ln1_g[l], ln1_b[l])
        moe = expert_choice_moe(x, w_router[l], w_gate[l], w_up[l], w_down[l])
        x = layer_norm(DN_ALPHA * x + moe, ln2_g[l], ln2_b[l])
    return x
```

```python
import functools
import math

import jax
import jax.numpy as jnp
import numpy as np
from jax import lax
from jax.experimental import pallas as pl
from jax.experimental.pallas import tpu as pltpu

F32 = jnp.float32
BF16 = jnp.bfloat16
I32 = jnp.int32

LANES = 128
HEAD_DIM = 64
NA_HEADS = 8
NA_KR = 8
NA_KC = 16
GRID_W = 64
NA_QROWS = 8
NA_KROWS = 16
WG_HEADS = 8
WG_KV_HEADS = 2
WINDOW = 128
WG_QBLK = 256
WG_KBLK = 512
N_EXPERTS = 16
EC_CAPACITY_FACTOR = 2
LN_EPS = 1e-5
NEG_INF = -1e30
MIN_NORMAL_BITS = 0x00800000
VMEM_LIMIT = 56 * 1024 * 1024


def _cparams(sem, vmem=VMEM_LIMIT):
    return pltpu.CompilerParams(dimension_semantics=sem, vmem_limit_bytes=vmem)


def _nt(a, b):
    return lax.dot_general(a, b, (((1,), (1,)), ((), ())), preferred_element_type=F32)


def _dot(a, b):
    return jnp.dot(a, b, preferred_element_type=F32)


def _qkv_kernel(x_ref, w_ref, b_ref, oa_ref, ob_ref, *, na_cols):
    x = x_ref[...].astype(BF16)
    acc = _dot(x, w_ref[...]) + b_ref[...]
    oa_ref[...] = acc[:, :na_cols].astype(BF16)
    ob_ref[...] = acc[:, na_cols:].astype(BF16)


def _qkv_proj(x2, w, b, na_cols, tm=1024):
    T, D = x2.shape
    N = w.shape[1]
    nb_cols = N - na_cols
    return pl.pallas_call(
        functools.partial(_qkv_kernel, na_cols=na_cols),
        grid=(T // tm,),
        in_specs=[
            pl.BlockSpec((tm, D), lambda i: (i, 0)),
            pl.BlockSpec((D, N), lambda i: (0, 0)),
            pl.BlockSpec((1, N), lambda i: (0, 0)),
        ],
        out_specs=[
            pl.BlockSpec((tm, na_cols), lambda i: (i, 0)),
            pl.BlockSpec((tm, nb_cols), lambda i: (i, 0)),
        ],
        out_shape=[
            jax.ShapeDtypeStruct((T, na_cols), BF16),
            jax.ShapeDtypeStruct((T, nb_cols), BF16),
        ],
        compiler_params=_cparams(("arbitrary",)),
        name="qkv_proj",
    )(x2, w, b)


def _na_bias_tables(rpb, rows):
    nblk = rows // NA_QROWS
    rq = np.arange(NA_QROWS)
    kr = np.arange(NA_KROWS)
    dr = np.zeros((3, NA_QROWS, NA_KROWS), np.int32)
    vr = np.zeros((3, NA_QROWS, NA_KROWS), bool)
    for cls, rb in enumerate((0, 1, nblk - 1)):
        kstart = int(np.clip(NA_QROWS * rb - NA_KR // 2, 0, rows - NA_KROWS))
        r = NA_QROWS * rb + rq
        r0 = np.clip(r - NA_KR // 2, 0, rows - NA_KR)
        krow = kstart + kr
        vr[cls] = (krow[None, :] >= r0[:, None]) & (krow[None, :] < r0[:, None] + NA_KR)
        dr[cls] = np.clip(krow[None, :] - r[:, None] + (NA_KR - 1), 0, 2 * NA_KR - 2)
    c = np.arange(GRID_W)
    c0 = np.clip(c - NA_KC // 2, 0, GRID_W - NA_KC)
    j = np.arange(GRID_W)
    vc = (j[None, :] >= c0[:, None]) & (j[None, :] < c0[:, None] + NA_KC)
    dc = np.clip(j[None, :] - c[:, None] + (NA_KC - 1), 0, 2 * NA_KC - 2)
    tmat = jnp.where(jnp.asarray(vc)[None, None], rpb[:, :, jnp.asarray(dc)], NEG_INF)
    full = tmat[:, jnp.asarray(dr)]
    full = jnp.where(jnp.asarray(vr)[None, :, :, :, None, None], full, NEG_INF)
    full = full.transpose(0, 1, 2, 4, 3, 5)
    H = rpb.shape[0]
    full = full.reshape(H // 2, 2, 3, NA_QROWS * GRID_W, NA_KROWS * GRID_W)
    return full.transpose(0, 2, 1, 3, 4).astype(F32)


def _na_kernel(q_ref, k0, k1, k2, k3, v0, v1, v2, v3, bias_ref, o_ref):
    q = q_ref[0]
    lane = lax.broadcasted_iota(I32, (1, LANES), 1)
    ks = (k0, k1, k2, k3)
    vs = (v0, v1, v2, v3)
    kc = ks[0].shape[1]
    out = jnp.zeros(q.shape, F32)
    for hh in range(2):
        hm = (lane >= hh * HEAD_DIM) & (lane < (hh + 1) * HEAD_DIM)
        qh = jnp.where(hm, q, jnp.zeros_like(q))
        s = []
        for i in range(4):
            si = _nt(qh, ks[i][0]) + bias_ref[0, 0, hh, :, i * kc:(i + 1) * kc]
            s.append(si)
        m = jnp.max(s[0], axis=1, keepdims=True)
        for i in range(1, 4):
            m = jnp.maximum(m, jnp.max(s[i], axis=1, keepdims=True))
        l = jnp.zeros_like(m)
        o = jnp.zeros(q.shape, F32)
        for i in range(4):
            e = jnp.exp(s[i] - m)
            l = l + jnp.sum(e, axis=1, keepdims=True)
            vi = vs[i][0]
            o = o + _dot(e.astype(BF16), jnp.where(hm, vi, jnp.zeros_like(vi)))
        out = out + o * (1.0 / l)
    o_ref[0] = out.astype(o_ref.dtype)


def _natten(qkv_a, bias_tab, B, S):
    rows = S // GRID_W
    nblk = rows // NA_QROWS
    qtok = NA_QROWS * GRID_W
    ktok = NA_KROWS * GRID_W // 4
    npair = NA_HEADS // 2
    kmax = (rows - NA_KROWS) * GRID_W // ktok

    def kblk(rb):
        return jnp.clip(2 * rb - 1, 0, kmax)

    def cls(rb):
        return jnp.where(rb == 0, 0, jnp.where(rb == nblk - 1, 2, 1))

    in_specs = [pl.BlockSpec((1, qtok, LANES), lambda h, rb, b: (b, rb, h))]
    for part in (1, 2):
        for i in range(4):
            in_specs.append(pl.BlockSpec(
                (1, ktok, LANES),
                lambda h, rb, b, part=part, i=i: (b, kblk(rb) + i, part * npair + h)))
    in_specs.append(pl.BlockSpec(
        (1, 1, 2, qtok, 4 * ktok), lambda h, rb, b: (h, cls(rb), 0, 0, 0)))
    return pl.pallas_call(
        _na_kernel,
        grid=(npair, nblk, B),
        in_specs=in_specs,
        out_specs=pl.BlockSpec((1, qtok, LANES), lambda h, rb, b: (b, rb, h)),
        out_shape=jax.ShapeDtypeStruct((B, S, NA_HEADS * HEAD_DIM), BF16),
        compiler_params=_cparams(("arbitrary", "arbitrary", "arbitrary")),
        name="natten",
    )(qkv_a, *([qkv_a] * 8), bias_tab)


WG_HEAD_ORDER = (0, 4, 1, 5, 2, 6, 3, 7)


def _wg_bias_tables(S):
    q = np.arange(WG_QBLK)[:, None]
    k = np.arange(WG_KBLK)[None, :]
    tabs = []
    for delta in (0, WINDOW, WG_KBLK - WG_QBLK):
        dist = np.abs(q - k + delta).astype(np.float32)
        tabs.append(dist)
    dist = jnp.asarray(np.stack(tabs))
    slopes = jnp.exp2(-8.0 * (jnp.arange(WG_HEADS, dtype=F32) + 1.0) / WG_HEADS)
    bias = -slopes[None, :, None, None] * dist[:, None]
    return jnp.where(dist[:, None] <= WINDOW, bias, NEG_INF).astype(F32)


def _wg_kernel(sink_ref, q_ref, k_ref, v_ref, bias_ref, o_ref, *, S):
    i = pl.program_id(1)
    kstart = pl.multiple_of(jnp.clip(i * WG_QBLK - WINDOW, 0, S - WG_KBLK), LANES)
    kw = k_ref[0, pl.ds(kstart, WG_KBLK), :]
    vw = v_ref[0, pl.ds(kstart, WG_KBLK), :]
    lane = lax.broadcasted_iota(I32, (1, LANES), 1)
    for p in range(WG_HEADS // 2):
        q2 = q_ref[0, :, p * LANES:(p + 1) * LANES]
        out = jnp.zeros(q2.shape, F32)
        for hh in range(2):
            head = WG_HEAD_ORDER[2 * p + hh]
            hm = (lane >= hh * HEAD_DIM) & (lane < (hh + 1) * HEAD_DIM)
            qh = jnp.where(hm, q2, jnp.zeros_like(q2))
            s = _nt(qh, kw) + bias_ref[0, head]
            snk = sink_ref[head]
            m = jnp.maximum(jnp.max(s, axis=1, keepdims=True), snk)
            e = jnp.exp(s - m)
            l = jnp.sum(e, axis=1, keepdims=True) + jnp.exp(snk - m)
            o = _dot(e.astype(BF16), jnp.where(hm, vw, jnp.zeros_like(vw)))
            out = out + o * (1.0 / l)
        o_ref[0, :, p * LANES:(p + 1) * LANES] = out.astype(o_ref.dtype)


def _wgqa(qkv_b, bias_tab, sink, B, S):
    nq = S // WG_QBLK
    qw = WG_HEADS * HEAD_DIM
    qblocks = qw // LANES

    def cls(i):
        return jnp.where(i == 0, 0, jnp.where(i == nq - 1, 2, 1))

    return pl.pallas_call(
        functools.partial(_wg_kernel, S=S),
        grid_spec=pltpu.PrefetchScalarGridSpec(
            num_scalar_prefetch=1,
            grid=(B, nq),
            in_specs=[
                pl.BlockSpec((1, WG_QBLK, qw), lambda b, i, s: (b, i, 0)),
                pl.BlockSpec((1, S, LANES), lambda b, i, s: (b, 0, qblocks)),
                pl.BlockSpec((1, S, LANES), lambda b, i, s: (b, 0, qblocks + 1)),
                pl.BlockSpec((1, WG_HEADS, WG_QBLK, WG_KBLK), lambda b, i, s: (cls(i), 0, 0, 0)),
            ],
            out_specs=pl.BlockSpec((1, WG_QBLK, qw), lambda b, i, s: (b, i, 0)),
        ),
        out_shape=jax.ShapeDtypeStruct((B, S, qw), BF16),
        compiler_params=_cparams(("arbitrary", "arbitrary")),
        name="wgqa",
    )(sink, qkv_b, qkv_b, qkv_b, bias_tab)


def _sigmoid(x):
    return 1.0 / (1.0 + jnp.exp(-x))


def _layer_norm(x, g, b):
    mu = jnp.mean(x, axis=-1, keepdims=True)
    xc = x - mu
    var = jnp.mean(xc * xc, axis=-1, keepdims=True)
    return xc * lax.rsqrt(var + LN_EPS) * g + b


def _merge_kernel(x_ref, ya_ref, yb_ref, wg_ref, bg_ref, wa_ref, wb_ref, wo_ref,
                  g_ref, b_ref, wrh_ref, wrl_ref,
                  h8_ref, aff2_ref, aff3_ref, *, alpha, sub):
    tm, D = x_ref.shape
    for st in range(tm // sub):
        r0 = st * sub
        x = x_ref[r0:r0 + sub, :]
        gates = _dot(x.astype(BF16), wg_ref[...]) + bg_ref[...]
        a = _dot(ya_ref[r0:r0 + sub, :], wa_ref[...])
        bm = _dot(yb_ref[r0:r0 + sub, :], wb_ref[...])
        mix = _sigmoid(gates[:, :D]) * a + _sigmoid(gates[:, D:]) * bm
        o = _dot(mix.astype(BF16), wo_ref[...])
        h = _layer_norm(alpha * x + o, g_ref[...], b_ref[...])
        for j in range(D // LANES):
            h8_ref[pl.ds(r0 * 8 + j, sub, stride=8), :] = h[:, j * LANES:(j + 1) * LANES]
        h_hi = h.astype(BF16)
        h_lo = (h - h_hi.astype(F32)).astype(BF16)
        lg = _nt(wrh_ref[...], h_hi) + (_nt(wrl_ref[...], h_hi) + _nt(wrh_ref[...], h_lo))
        mx = jnp.max(lg, axis=0, keepdims=True)
        ex = jnp.exp(lg - mx)
        aff = ex / jnp.sum(ex, axis=0, keepdims=True)
        aff2_ref[0, :, r0:r0 + sub] = aff
        for c in range(sub // LANES):
            cc = r0 // LANES + c
            for e in range(N_EXPERTS):
                aff3_ref[0, e, cc:cc + 1, :] = aff[e:e + 1, c * LANES:(c + 1) * LANES]


def _merge(x2, ya, yb, wg, bg, wa, wb, wo, g1, b1, wr_hi, wr_lo, B, S, alpha, tm=1024, sub=256):
    T, D = x2.shape
    nt = S // tm
    E = N_EXPERTS
    const = lambda i: (0, 0)
    return pl.pallas_call(
        functools.partial(_merge_kernel, alpha=alpha, sub=sub),
        grid=(T // tm,),
        in_specs=[
            pl.BlockSpec((tm, D), lambda i: (i, 0)),
            pl.BlockSpec((tm, ya.shape[1]), lambda i: (i, 0)),
            pl.BlockSpec((tm, yb.shape[1]), lambda i: (i, 0)),
            pl.BlockSpec(wg.shape, const),
            pl.BlockSpec(bg.shape, const),
            pl.BlockSpec(wa.shape, const),
            pl.BlockSpec(wb.shape, const),
            pl.BlockSpec(wo.shape, const),
            pl.BlockSpec(g1.shape, const),
            pl.BlockSpec(b1.shape, const),
            pl.BlockSpec(wr_hi.shape, const),
            pl.BlockSpec(wr_lo.shape, const),
        ],
        out_specs=[
            pl.BlockSpec((tm * 8, LANES), lambda i: (i, 0)),
            pl.BlockSpec((1, E, tm), lambda i: (i // nt, 0, i % nt)),
            pl.BlockSpec((1, E, tm // LANES, LANES), lambda i: (i // nt, 0, i % nt, 0)),
        ],
        out_shape=[
            jax.ShapeDtypeStruct((T * 8, LANES), F32),
            jax.ShapeDtypeStruct((B, E, S), F32),
            jax.ShapeDtypeStruct((B, E, S // LANES, LANES), F32),
        ],
        compiler_params=_cparams(("arbitrary",)),
        name="merge_ln_router",
    )(x2, ya, yb, wg, bg, wa, wb, wo, g1, b1, wr_hi, wr_lo)


def _onehot(cond):
    return jnp.where(cond, 1.0, 0.0).astype(BF16)


def _route_kernel(aff2_ref, aff3_ref, idx_ref, gate_ref, ct_ref, *, cap):
    E, nch = aff3_ref.shape[1], aff3_ref.shape[2]
    capf = float(cap)
    a2 = aff2_ref[0]

    def bit_step(it, prefix):
        cand = prefix | jnp.left_shift(jnp.int32(1), 30 - it)
        cnt = jnp.sum(jnp.where(a2 >= lax.bitcast_convert_type(cand, F32), 1.0, 0.0),
                      axis=1, keepdims=True)
        return jnp.where((cnt >= capf) & (cand >= MIN_NORMAL_BITS), cand, prefix)

    thr = lax.bitcast_convert_type(lax.fori_loop(0, 31, bit_step, jnp.zeros((E, 1), I32)), F32)
    n_gt = jnp.sum(jnp.where(a2 > thr, 1.0, 0.0), axis=1, keepdims=True)
    need = capf - n_gt

    ri = lax.broadcasted_iota(I32, (LANES, LANES), 0)
    ci = lax.broadcasted_iota(I32, (LANES, LANES), 1)
    upper = _onehot(ri <= ci)
    ones_l = jnp.ones((LANES, LANES), BF16)
    rc = lax.broadcasted_iota(I32, (nch, nch), 0)
    cc = lax.broadcasted_iota(I32, (nch, nch), 1)
    lower_strict = _onehot(cc < rc)
    upper_c = _onehot(rc <= cc)
    ones_c = jnp.ones((nch, nch), BF16)
    crow = lax.broadcasted_iota(I32, (nch, LANES), 0)
    s_b = lax.broadcasted_iota(I32, (cap, LANES), 0).astype(F32)
    s_c = lax.broadcasted_iota(I32, (cap, nch), 0).astype(F32)
    chunk_id = lax.broadcasted_iota(I32, (cap, nch), 1).astype(F32)
    lane_id = lax.broadcasted_iota(I32, (cap, LANES), 1).astype(F32)

    def prefix(mask_b):
        local = _dot(mask_b, upper)
        tot_b = _dot(mask_b, ones_l).astype(BF16)
        return local, _dot(lower_strict, tot_b)

    for e in range(E):
        xe = aff3_ref[0, e]
        te = thr[e:e + 1, :]
        gt = xe > te
        eq = xe == te
        eq_f = jnp.where(eq, 1.0, 0.0)
        loc_eq, off_eq = prefix(eq_f.astype(BF16))
        rank_excl = loc_eq + off_eq - eq_f
        sel = gt | (eq & (rank_excl < need[e:e + 1, :]))
        sel_b = _onehot(sel)
        loc = _dot(sel_b, upper)
        tot_r = _nt(jnp.ones((LANES, LANES), BF16), sel_b)
        tot_rb = tot_r.astype(BF16)
        ct_r = _dot(tot_rb, upper_c)
        ct_ref[0, e:e + 1, :] = ct_r[0:1, :].astype(I32)
        g = _onehot(ct_r[0:1, :] <= s_c)
        off_b = _nt(g, tot_rb)
        j_b = _dot(g, ones_c)
        j_r = _nt(jnp.ones((8, nch), BF16), g)
        nxt = pltpu.roll(loc, nch - 1, 0)
        dif = jnp.where(crow < nch - 1, nxt - loc, 0.0).astype(BF16)
        lg = loc[0:1, :] + _dot(g, dif)
        ind = _onehot(lg <= s_b - off_b)
        loc_b = _dot(ind, ones_l)
        loc_r = _nt(jnp.ones((8, LANES), BF16), ind)
        idx_ref[0, e:e + 1, :] = (j_r[0:1, :] * float(LANES) + loc_r[0:1, :]).astype(I32)
        oh = _onehot(chunk_id == j_b)
        pick = lane_id == loc_b
        x_hi = xe.astype(BF16)
        r1 = xe - x_hi.astype(F32)
        x_mid = r1.astype(BF16)
        x_lo = (r1 - x_mid.astype(F32)).astype(BF16)
        ones8 = jnp.ones((8, LANES), BF16)
        gate = None
        for piece in (x_hi, x_mid, x_lo):
            rows = _dot(oh, piece)
            val = _nt(ones8, jnp.where(pick, rows, 0.0).astype(BF16))
            gate = val if gate is None else gate + val
        gate_ref[0, e:e + 1, :] = gate[0:1, :]


def _route(aff2, aff3, cap):
    B, E, S = aff2.shape
    nch = S // LANES
    return pl.pallas_call(
        functools.partial(_route_kernel, cap=cap),
        grid=(B,),
        in_specs=[
            pl.BlockSpec((1, E, S), lambda b: (b, 0, 0)),
            pl.BlockSpec((1, E, nch, LANES), lambda b: (b, 0, 0, 0)),
        ],
        out_specs=[
            pl.BlockSpec((1, E, cap), lambda b: (b, 0, 0)),
            pl.BlockSpec((1, E, cap), lambda b: (b, 0, 0)),
            pl.BlockSpec((1, E, nch), lambda b: (b, 0, 0)),
        ],
        out_shape=[
            jax.ShapeDtypeStruct((B, E, cap), I32),
            jax.ShapeDtypeStruct((B, E, cap), F32),
            jax.ShapeDtypeStruct((B, E, nch), I32),
        ],
        compiler_params=_cparams(("arbitrary",)),
        name="route",
    )(aff2, aff3)


def _gather_kernel(idx_ref, h8_hbm, xe_ref, hbuf, xr, sem, *, cap, unroll):
    b = pl.program_id(0)
    e = pl.program_id(1)
    D = xe_ref.shape[3]

    @pl.when(e == 0)
    def _load_tokens():
        cp = pltpu.make_async_copy(h8_hbm.at[b], hbuf, sem)
        cp.start()
        cp.wait()

    def body(k, carry):
        base = k * unroll
        slabs = []
        for u in range(unroll):
            t = idx_ref[0, 0, base + u]
            slabs.append(hbuf[pl.ds(pl.multiple_of(t * 8, 8), 8), :])
        for u in range(unroll):
            xr[pl.ds(pl.multiple_of((base + u) * 8, 8), 8), :] = slabs[u]
        return carry

    lax.fori_loop(0, cap // unroll, body, 0)
    for j in range(D // LANES):
        xe_ref[0, 0, :, j * LANES:(j + 1) * LANES] = xr[pl.ds(j, cap, stride=8), :].astype(xe_ref.dtype)


def _gather_tokens(idx3, h8, B, S, D, cap, unroll=8):
    E = N_EXPERTS
    return pl.pallas_call(
        functools.partial(_gather_kernel, cap=cap, unroll=unroll),
        grid=(B, E),
        in_specs=[
            pl.BlockSpec((1, 1, cap), lambda b, e: (b * E + e, 0, 0),
                         memory_space=pltpu.MemorySpace.SMEM),
            pl.BlockSpec(memory_space=pl.ANY),
        ],
        out_specs=pl.BlockSpec((1, 1, cap, D), lambda b, e: (b, e, 0, 0)),
        out_shape=jax.ShapeDtypeStruct((B, E, cap, D), BF16),
        scratch_shapes=[
            pltpu.VMEM((S * 8, LANES), F32),
            pltpu.VMEM((cap * 8, LANES), F32),
            pltpu.SemaphoreType.DMA(()),
        ],
        compiler_params=_cparams(("arbitrary", "arbitrary")),
        name="gather_tokens",
    )(idx3, h8)


def _ffn_kernel(xe_ref, wg_ref, wu_ref, wd_ref, y_ref, *, sub):
    cap, D = xe_ref.shape[2], xe_ref.shape[3]
    for st in range(cap // sub):
        xs = xe_ref[0, 0, st * sub:(st + 1) * sub, :]
        g = _dot(xs, wg_ref[0])
        u = _dot(xs, wu_ref[0])
        hid = (g * _sigmoid(g)) * u
        y = _dot(hid.astype(BF16), wd_ref[0])
        for j in range(D // LANES):
            y_ref[0, 0, pl.ds(st * sub * 8 + j, sub, stride=8), :] = y[:, j * LANES:(j + 1) * LANES]


def _ffn(xe, wg, wu, wd, sub=256):
    B, E, cap, D = xe.shape
    FF = wg.shape[2]
    return pl.pallas_call(
        functools.partial(_ffn_kernel, sub=sub),
        grid=(E, B),
        in_specs=[
            pl.BlockSpec((1, 1, cap, D), lambda e, b: (b, e, 0, 0)),
            pl.BlockSpec((1, D, FF), lambda e, b: (e, 0, 0)),
            pl.BlockSpec((1, D, FF), lambda e, b: (e, 0, 0)),
            pl.BlockSpec((1, FF, D), lambda e, b: (e, 0, 0)),
        ],
        out_specs=pl.BlockSpec((1, 1, cap * 8, LANES), lambda e, b: (b, e, 0, 0)),
        out_shape=jax.ShapeDtypeStruct((B, E, cap * 8, LANES), F32),
        compiler_params=_cparams(("arbitrary", "arbitrary")),
        name="expert_ffn",
    )(xe, wg, wu, wd)


def _combine_kernel(ct_ref, idx_ref, gate_ref, y_ref, h8_ref, g_ref, b_ref, o_ref, acc,
                    *, alpha, nch, nq, sub):
    b = pl.program_id(0)
    q = pl.program_id(1)
    e = pl.program_id(2)
    ne = pl.num_programs(2)
    tq = o_ref.shape[1]
    D = o_ref.shape[2]
    cpq = nch // nq
    rows_blk = 2048

    @pl.when(e == 0)
    def _init():
        def body(k, carry):
            r = pl.multiple_of(k * rows_blk, rows_blk)
            acc[pl.ds(r, rows_blk), :] = alpha * h8_ref[0, pl.ds(r, rows_blk), :]
            return carry
        lax.fori_loop(0, tq * 8 // rows_blk, body, 0)

    base = (b * ne + e) * nch
    hi = ct_ref[base + (q + 1) * cpq - 1]
    lo = jnp.where(q == 0, 0, ct_ref[base + jnp.maximum(q * cpq - 1, 0)])
    tok0 = q * tq

    def body(i, carry):
        t = idx_ref[0, 0, i] - tok0
        gt = gate_ref[0, 0, i]
        dst = pl.multiple_of(t * 8, 8)
        src = pl.multiple_of(i * 8, 8)
        acc[pl.ds(dst, 8), :] = acc[pl.ds(dst, 8), :] + gt * y_ref[0, 0, pl.ds(src, 8), :]
        return carry

    lax.fori_loop(lo, hi, body, 0)

    @pl.when(e == ne - 1)
    def _finish():
        for st in range(tq // sub):
            parts = [acc[pl.ds(st * sub * 8 + j, sub, stride=8), :] for j in range(D // LANES)]
            x = jnp.concatenate(parts, axis=1)
            o_ref[0, st * sub:(st + 1) * sub, :] = _layer_norm(x, g_ref[...], b_ref[...])


def _combine(ct_flat, idx3, gate3, y, h8, g2, b2, B, S, D, cap, alpha, nq=4, sub=256):
    E = N_EXPERTS
    nch = S // LANES
    tq = S // nq
    return pl.pallas_call(
        functools.partial(_combine_kernel, alpha=alpha, nch=nch, nq=nq, sub=sub),
        grid_spec=pltpu.PrefetchScalarGridSpec(
            num_scalar_prefetch=1,
            grid=(B, nq, E),
            in_specs=[
                pl.BlockSpec((1, 1, cap), lambda b, q, e, ct: (b * E + e, 0, 0),
                             memory_space=pltpu.MemorySpace.SMEM),
                pl.BlockSpec((1, 1, cap), lambda b, q, e, ct: (b * E + e, 0, 0),
                             memory_space=pltpu.MemorySpace.SMEM),
                pl.BlockSpec((1, 1, cap * 8, LANES), lambda b, q, e, ct: (b, e, 0, 0)),
                pl.BlockSpec((1, tq * 8, LANES), lambda b, q, e, ct: (b, q, 0)),
                pl.BlockSpec((1, D), lambda b, q, e, ct: (0, 0)),
                pl.BlockSpec((1, D), lambda b, q, e, ct: (0, 0)),
            ],
            out_specs=pl.BlockSpec((1, tq, D), lambda b, q, e, ct: (b, q, 0)),
            scratch_shapes=[pltpu.VMEM((tq * 8, LANES), F32)],
        ),
        out_shape=jax.ShapeDtypeStruct((B, S, D), F32),
        compiler_params=_cparams(("arbitrary", "arbitrary", "arbitrary")),
        name="combine_ln",
    )(ct_flat, idx3, gate3, y, h8, g2, b2)


def _layer(x, w_in, b_in, rpb, sink, w_branch_a, w_branch_b, w_out, ln1_g, ln1_b,
           w_router, w_gate, w_up, w_down, ln2_g, ln2_b, alpha):
    B, S, D = x.shape
    T = B * S
    na_w = NA_HEADS * HEAD_DIM
    wq_w = WG_HEADS * HEAD_DIM
    kv_w = WG_KV_HEADS * HEAD_DIM
    o = np.cumsum([0, na_w, na_w, na_w, wq_w, kv_w, kv_w, D, D])
    scale = HEAD_DIM ** -0.5
    perm = np.concatenate([np.arange(h * HEAD_DIM, (h + 1) * HEAD_DIM) for h in WG_HEAD_ORDER])

    def cols(a):
        qa = a[..., o[0]:o[1]] * scale
        qb = (a[..., o[3]:o[4]] * scale)[..., perm]
        return jnp.concatenate([qa, a[..., o[1]:o[3]], qb, a[..., o[4]:o[6]]], axis=-1)

    w_qkv = cols(w_in).astype(BF16)
    b_qkv = cols(b_in)[None, :]
    w_g = w_in[:, o[6]:].astype(BF16)
    b_g = b_in[None, o[6]:]

    x2 = x.reshape(T, D)
    qkv_a, qkv_b = _qkv_proj(x2, w_qkv, b_qkv, 3 * na_w)
    ya = _natten(qkv_a.reshape(B, S, 3 * na_w), _na_bias_tables(rpb, S // GRID_W), B, S)
    yb = _wgqa(qkv_b.reshape(B, S, wq_w + 2 * kv_w), _wg_bias_tables(S), sink, B, S)

    wr_t = w_router.T
    wr_hi = wr_t.astype(BF16)
    wr_lo = (wr_t - wr_hi.astype(F32)).astype(BF16)
    h8, aff2, aff3 = _merge(
        x2, ya.reshape(T, na_w), yb.reshape(T, wq_w), w_g, b_g,
        w_branch_a.astype(BF16), w_branch_b[perm].astype(BF16), w_out.astype(BF16),
        ln1_g[None, :], ln1_b[None, :], wr_hi, wr_lo, B, S, alpha)

    cap = EC_CAPACITY_FACTOR * S // N_EXPERTS
    idx, gate, ct = _route(aff2, aff3, cap)
    idx3 = idx.reshape(B * N_EXPERTS, 1, cap)
    gate3 = gate.reshape(B * N_EXPERTS, 1, cap)
    xe = _gather_tokens(idx3, h8.reshape(B, S * 8, LANES), B, S, D, cap)
    y = _ffn(xe, w_gate.astype(BF16), w_up.astype(BF16), w_down.astype(BF16))
    return _combine(ct.reshape(-1), idx3, gate3, y, h8.reshape(B, S * 8, LANES),
                    ln2_g[None, :], ln2_b[None, :], B, S, D, cap, alpha)


def kernel(x, w_in, b_in, rpb, sink, w_branch_a, w_branch_b, w_out, ln1_g, ln1_b,
           w_router, w_gate, w_up, w_down, ln2_g, ln2_b):
    depth = w_in.shape[0]
    alpha = (2 * depth) ** 0.25
    for l in range(depth):
        x = _layer(x, w_in[l], b_in[l], rpb[l], sink[l], w_branch_a[l], w_branch_b[l],
                   w_out[l], ln1_g[l], ln1_b[l], w_router[l], w_gate[l], w_up[l],
                   w_down[l], ln2_g[l], ln2_b[l], alpha)
    return x
```

```python
import functools
import math

import jax
import jax.numpy as jnp
import numpy as np
from jax import lax
from jax.experimental import pallas as pl
from jax.experimental.pallas import tpu as pltpu

F32 = jnp.float32
BF16 = jnp.bfloat16
I32 = jnp.int32

LANES = 128
HEAD_DIM = 64
NA_HEADS = 8
NA_KR = 8
NA_KC = 16
GRID_W = 64
NA_GROWS = 4
NA_GROUPS = 2
NA_KCHUNKS = 3
WG_HEADS = 8
WG_KV_HEADS = 2
WINDOW = 128
WG_QBLK = 256
WG_KBLK = 512
N_EXPERTS = 16
EC_CAPACITY_FACTOR = 2
LN_EPS = 1e-5
NEG_INF = -1e30
MIN_NORMAL_BITS = 0x00800000
VMEM_LIMIT = 56 * 1024 * 1024


def _cparams(sem, vmem=VMEM_LIMIT):
    return pltpu.CompilerParams(dimension_semantics=sem, vmem_limit_bytes=vmem)


def _nt(a, b):
    return lax.dot_general(a, b, (((1,), (1,)), ((), ())), preferred_element_type=F32)


def _dot(a, b):
    return jnp.dot(a, b, preferred_element_type=F32)


def _qkv_kernel(x_ref, w_ref, b_ref, oa_ref, ob_ref, *, na_cols):
    x = x_ref[...].astype(BF16)
    acc = _dot(x, w_ref[...]) + b_ref[...]
    oa_ref[...] = acc[:, :na_cols].astype(BF16)
    ob_ref[...] = acc[:, na_cols:].astype(BF16)


def _qkv_proj(x2, w, b, na_cols, tm=1024):
    T, D = x2.shape
    N = w.shape[1]
    nb_cols = N - na_cols
    return pl.pallas_call(
        functools.partial(_qkv_kernel, na_cols=na_cols),
        grid=(T // tm,),
        in_specs=[
            pl.BlockSpec((tm, D), lambda i: (i, 0)),
            pl.BlockSpec((D, N), lambda i: (0, 0)),
            pl.BlockSpec((1, N), lambda i: (0, 0)),
        ],
        out_specs=[
            pl.BlockSpec((tm, na_cols), lambda i: (i, 0)),
            pl.BlockSpec((tm, nb_cols), lambda i: (i, 0)),
        ],
        out_shape=[
            jax.ShapeDtypeStruct((T, na_cols), BF16),
            jax.ShapeDtypeStruct((T, nb_cols), BF16),
        ],
        compiler_params=_cparams(("arbitrary",)),
        name="qkv_proj",
    )(x2, w, b)


def _na_bias_tables(rpb, rows):
    ngrp = rows // NA_GROWS
    nkr = NA_KCHUNKS * NA_GROWS
    rq = np.arange(NA_GROWS)
    kr = np.arange(nkr)
    dr = np.zeros((3, NA_GROWS, nkr), np.int32)
    vr = np.zeros((3, NA_GROWS, nkr), bool)
    for cls, gi in enumerate((0, 1, ngrp - 1)):
        kstart = int(np.clip(NA_GROWS * gi - NA_KR // 2, 0, rows - nkr))
        r = NA_GROWS * gi + rq
        r0 = np.clip(r - NA_KR // 2, 0, rows - NA_KR)
        krow = kstart + kr
        vr[cls] = (krow[None, :] >= r0[:, None]) & (krow[None, :] < r0[:, None] + NA_KR)
        dr[cls] = np.clip(krow[None, :] - r[:, None] + (NA_KR - 1), 0, 2 * NA_KR - 2)
    c = np.arange(GRID_W)
    c0 = np.clip(c - NA_KC // 2, 0, GRID_W - NA_KC)
    j = np.arange(GRID_W)
    vc = (j[None, :] >= c0[:, None]) & (j[None, :] < c0[:, None] + NA_KC)
    dc = np.clip(j[None, :] - c[:, None] + (NA_KC - 1), 0, 2 * NA_KC - 2)
    tmat = jnp.where(jnp.asarray(vc)[None, None], rpb[:, :, jnp.asarray(dc)], NEG_INF)
    full = tmat[:, jnp.asarray(dr)]
    full = jnp.where(jnp.asarray(vr)[None, :, :, :, None, None], full, NEG_INF)
    full = full.transpose(0, 1, 2, 4, 3, 5)
    H = rpb.shape[0]
    full = full.reshape(H // 2, 2, 3, NA_GROWS * GRID_W, nkr * GRID_W)
    return full.transpose(0, 2, 1, 3, 4).astype(F32)


def _na_kernel(q_ref, *refs):
    nkv = NA_GROUPS * NA_KCHUNKS
    k_refs, v_refs = refs[:nkv], refs[nkv:2 * nkv]
    bias_refs = refs[2 * nkv:2 * nkv + NA_GROUPS]
    o_ref = refs[2 * nkv + NA_GROUPS]
    gtok = NA_GROWS * GRID_W
    lane = lax.broadcasted_iota(I32, (1, LANES), 1)
    for g in range(NA_GROUPS):
        q = q_ref[0, g * gtok:(g + 1) * gtok, :]
        ks = k_refs[g * NA_KCHUNKS:(g + 1) * NA_KCHUNKS]
        vs = v_refs[g * NA_KCHUNKS:(g + 1) * NA_KCHUNKS]
        kc = ks[0].shape[1]
        out = jnp.zeros(q.shape, F32)
        for hh in range(2):
            hm = (lane >= hh * HEAD_DIM) & (lane < (hh + 1) * HEAD_DIM)
            qh = jnp.where(hm, q, jnp.zeros_like(q))
            s = [_nt(qh, ks[i][0]) + bias_refs[g][0, 0, hh, :, i * kc:(i + 1) * kc]
                 for i in range(NA_KCHUNKS)]
            m = jnp.max(s[0], axis=1, keepdims=True)
            for i in range(1, NA_KCHUNKS):
                m = jnp.maximum(m, jnp.max(s[i], axis=1, keepdims=True))
            l = jnp.zeros_like(m)
            o = jnp.zeros(q.shape, F32)
            for i in range(NA_KCHUNKS):
                e = jnp.exp(s[i] - m)
                l = l + jnp.sum(e, axis=1, keepdims=True)
                vi = vs[i][0]
                o = o + _dot(e.astype(BF16), jnp.where(hm, vi, jnp.zeros_like(vi)))
            out = out + o * (1.0 / l)
        o_ref[0, g * gtok:(g + 1) * gtok, :] = out.astype(o_ref.dtype)


def _natten(qkv_a, bias_tab, B, S):
    rows = S // GRID_W
    ngrp = rows // NA_GROWS
    nblk = ngrp // NA_GROUPS
    gtok = NA_GROWS * GRID_W
    npair = NA_HEADS // 2
    cmax = ngrp - NA_KCHUNKS

    def kchunk(rb, g):
        return jnp.clip(NA_GROUPS * rb + g - 1, 0, cmax)

    def cls(rb, g):
        gi = NA_GROUPS * rb + g
        return jnp.where(gi == 0, 0, jnp.where(gi == ngrp - 1, 2, 1))

    in_specs = [pl.BlockSpec((1, NA_GROUPS * gtok, LANES), lambda h, rb, b: (b, rb, h))]
    for part in (1, 2):
        for g in range(NA_GROUPS):
            for i in range(NA_KCHUNKS):
                in_specs.append(pl.BlockSpec(
                    (1, gtok, LANES),
                    lambda h, rb, b, part=part, g=g, i=i: (b, kchunk(rb, g) + i, part * npair + h)))
    for g in range(NA_GROUPS):
        in_specs.append(pl.BlockSpec(
            (1, 1, 2, gtok, NA_KCHUNKS * gtok), lambda h, rb, b, g=g: (h, cls(rb, g), 0, 0, 0)))
    nkv = NA_GROUPS * NA_KCHUNKS
    return pl.pallas_call(
        _na_kernel,
        grid=(npair, nblk, B),
        in_specs=in_specs,
        out_specs=pl.BlockSpec((1, NA_GROUPS * gtok, LANES), lambda h, rb, b: (b, rb, h)),
        out_shape=jax.ShapeDtypeStruct((B, S, NA_HEADS * HEAD_DIM), BF16),
        compiler_params=_cparams(("arbitrary", "arbitrary", "arbitrary")),
        name="natten",
    )(qkv_a, *([qkv_a] * (2 * nkv)), *([bias_tab] * NA_GROUPS))


WG_HEAD_ORDER = (0, 4, 1, 5, 2, 6, 3, 7)


def _wg_bias_tables(S):
    q = np.arange(WG_QBLK)[:, None]
    k = np.arange(WG_KBLK)[None, :]
    tabs = []
    for delta in (0, WINDOW, WG_KBLK - WG_QBLK):
        dist = np.abs(q - k + delta).astype(np.float32)
        tabs.append(dist)
    dist = jnp.asarray(np.stack(tabs))
    slopes = jnp.exp2(-8.0 * (jnp.arange(WG_HEADS, dtype=F32) + 1.0) / WG_HEADS)
    bias = -slopes[None, :, None, None] * dist[:, None]
    return jnp.where(dist[:, None] <= WINDOW, bias, NEG_INF).astype(F32)


def _wg_kernel(sink_ref, q_ref, k_ref, v_ref, bias_ref, o_ref, *, S):
    i = pl.program_id(1)
    kstart = pl.multiple_of(jnp.clip(i * WG_QBLK - WINDOW, 0, S - WG_KBLK), LANES)
    kw = k_ref[0, pl.ds(kstart, WG_KBLK), :]
    vw = v_ref[0, pl.ds(kstart, WG_KBLK), :]
    lane = lax.broadcasted_iota(I32, (1, LANES), 1)
    for p in range(WG_HEADS // 2):
        q2 = q_ref[0, :, p * LANES:(p + 1) * LANES]
        out = jnp.zeros(q2.shape, F32)
        for hh in range(2):
            head = WG_HEAD_ORDER[2 * p + hh]
            hm = (lane >= hh * HEAD_DIM) & (lane < (hh + 1) * HEAD_DIM)
            qh = jnp.where(hm, q2, jnp.zeros_like(q2))
            s = _nt(qh, kw) + bias_ref[0, head]
            snk = sink_ref[head]
            m = jnp.maximum(jnp.max(s, axis=1, keepdims=True), snk)
            e = jnp.exp(s - m)
            l = jnp.sum(e, axis=1, keepdims=True) + jnp.exp(snk - m)
            o = _dot(e.astype(BF16), jnp.where(hm, vw, jnp.zeros_like(vw)))
            out = out + o * (1.0 / l)
        o_ref[0, :, p * LANES:(p + 1) * LANES] = out.astype(o_ref.dtype)


def _wgqa(qkv_b, bias_tab, sink, B, S):
    nq = S // WG_QBLK
    qw = WG_HEADS * HEAD_DIM
    qblocks = qw // LANES

    def cls(i):
        return jnp.where(i == 0, 0, jnp.where(i == nq - 1, 2, 1))

    return pl.pallas_call(
        functools.partial(_wg_kernel, S=S),
        grid_spec=pltpu.PrefetchScalarGridSpec(
            num_scalar_prefetch=1,
            grid=(B, nq),
            in_specs=[
                pl.BlockSpec((1, WG_QBLK, qw), lambda b, i, s: (b, i, 0)),
                pl.BlockSpec((1, S, LANES), lambda b, i, s: (b, 0, qblocks)),
                pl.BlockSpec((1, S, LANES), lambda b, i, s: (b, 0, qblocks + 1)),
                pl.BlockSpec((1, WG_HEADS, WG_QBLK, WG_KBLK), lambda b, i, s: (cls(i), 0, 0, 0)),
            ],
            out_specs=pl.BlockSpec((1, WG_QBLK, qw), lambda b, i, s: (b, i, 0)),
        ),
        out_shape=jax.ShapeDtypeStruct((B, S, qw), BF16),
        compiler_params=_cparams(("arbitrary", "arbitrary")),
        name="wgqa",
    )(sink, qkv_b, qkv_b, qkv_b, bias_tab)


def _sigmoid(x):
    return 1.0 / (1.0 + jnp.exp(-x))


def _layer_norm(x, g, b):
    mu = jnp.mean(x, axis=-1, keepdims=True)
    xc = x - mu
    var = jnp.mean(xc * xc, axis=-1, keepdims=True)
    return xc * lax.rsqrt(var + LN_EPS) * g + b


def _merge_kernel(x_ref, ya_ref, yb_ref, wg_ref, bg_ref, wa_ref, wb_ref, wo_ref,
                  g_ref, b_ref, wrh_ref, wrl_ref,
                  h8_ref, aff2_ref, aff3_ref, *, alpha, sub):
    tm, D = x_ref.shape
    for st in range(tm // sub):
        r0 = st * sub
        x = x_ref[r0:r0 + sub, :]
        gates = _dot(x.astype(BF16), wg_ref[...]) + bg_ref[...]
        a = _dot(ya_ref[r0:r0 + sub, :], wa_ref[...])
        bm = _dot(yb_ref[r0:r0 + sub, :], wb_ref[...])
        mix = _sigmoid(gates[:, :D]) * a + _sigmoid(gates[:, D:]) * bm
        o = _dot(mix.astype(BF16), wo_ref[...])
        h = _layer_norm(alpha * x + o, g_ref[...], b_ref[...])
        for j in range(D // LANES):
            h8_ref[pl.ds(r0 * 8 + j, sub, stride=8), :] = h[:, j * LANES:(j + 1) * LANES]
        h_hi = h.astype(BF16)
        h_lo = (h - h_hi.astype(F32)).astype(BF16)
        lg = _nt(wrh_ref[...], h_hi) + (_nt(wrl_ref[...], h_hi) + _nt(wrh_ref[...], h_lo))
        mx = jnp.max(lg, axis=0, keepdims=True)
        ex = jnp.exp(lg - mx)
        aff = ex / jnp.sum(ex, axis=0, keepdims=True)
        aff2_ref[0, :, r0:r0 + sub] = aff
        for c in range(sub // LANES):
            cc = r0 // LANES + c
            for e in range(N_EXPERTS):
                aff3_ref[0, e, cc:cc + 1, :] = aff[e:e + 1, c * LANES:(c + 1) * LANES]


def _merge(x2, ya, yb, wg, bg, wa, wb, wo, g1, b1, wr_hi, wr_lo, B, S, alpha, tm=1024, sub=256):
    T, D = x2.shape
    nt = S // tm
    E = N_EXPERTS
    const = lambda i: (0, 0)
    return pl.pallas_call(
        functools.partial(_merge_kernel, alpha=alpha, sub=sub),
        grid=(T // tm,),
        in_specs=[
            pl.BlockSpec((tm, D), lambda i: (i, 0)),
            pl.BlockSpec((tm, ya.shape[1]), lambda i: (i, 0)),
            pl.BlockSpec((tm, yb.shape[1]), lambda i: (i, 0)),
            pl.BlockSpec(wg.shape, const),
            pl.BlockSpec(bg.shape, const),
            pl.BlockSpec(wa.shape, const),
            pl.BlockSpec(wb.shape, const),
            pl.BlockSpec(wo.shape, const),
            pl.BlockSpec(g1.shape, const),
            pl.BlockSpec(b1.shape, const),
            pl.BlockSpec(wr_hi.shape, const),
            pl.BlockSpec(wr_lo.shape, const),
        ],
        out_specs=[
            pl.BlockSpec((tm * 8, LANES), lambda i: (i, 0)),
            pl.BlockSpec((1, E, tm), lambda i: (i // nt, 0, i % nt)),
            pl.BlockSpec((1, E, tm // LANES, LANES), lambda i: (i // nt, 0, i % nt, 0)),
        ],
        out_shape=[
            jax.ShapeDtypeStruct((T * 8, LANES), F32),
            jax.ShapeDtypeStruct((B, E, S), F32),
            jax.ShapeDtypeStruct((B, E, S // LANES, LANES), F32),
        ],
        compiler_params=_cparams(("arbitrary",)),
        name="merge_ln_router",
    )(x2, ya, yb, wg, bg, wa, wb, wo, g1, b1, wr_hi, wr_lo)


def _onehot(cond):
    return jnp.where(cond, 1.0, 0.0).astype(BF16)


def _route_kernel(aff2_ref, aff3_ref, idx_ref, gate_ref, *, cap):
    E, nch = aff3_ref.shape[1], aff3_ref.shape[2]
    capf = float(cap)
    a2 = aff2_ref[0]

    def bit_step(it, prefix):
        cand = prefix | jnp.left_shift(jnp.int32(1), 30 - it)
        cnt = jnp.sum(jnp.where(a2 >= lax.bitcast_convert_type(cand, F32), 1.0, 0.0),
                      axis=1, keepdims=True)
        return jnp.where((cnt >= capf) & (cand >= MIN_NORMAL_BITS), cand, prefix)

    thr = lax.bitcast_convert_type(lax.fori_loop(0, 31, bit_step, jnp.zeros((E, 1), I32)), F32)
    n_gt = jnp.sum(jnp.where(a2 > thr, 1.0, 0.0), axis=1, keepdims=True)
    need = capf - n_gt

    ri = lax.broadcasted_iota(I32, (LANES, LANES), 0)
    ci = lax.broadcasted_iota(I32, (LANES, LANES), 1)
    upper = _onehot(ri <= ci)
    ones_l = jnp.ones((LANES, LANES), BF16)
    rc = lax.broadcasted_iota(I32, (nch, nch), 0)
    cc = lax.broadcasted_iota(I32, (nch, nch), 1)
    lower_strict = _onehot(cc < rc)
    upper_c = _onehot(rc <= cc)
    ones_c = jnp.ones((nch, nch), BF16)
    crow = lax.broadcasted_iota(I32, (nch, LANES), 0)
    s_b = lax.broadcasted_iota(I32, (cap, LANES), 0).astype(F32)
    s_c = lax.broadcasted_iota(I32, (cap, nch), 0).astype(F32)
    chunk_id = lax.broadcasted_iota(I32, (cap, nch), 1).astype(F32)
    lane_id = lax.broadcasted_iota(I32, (cap, LANES), 1).astype(F32)

    def prefix(mask_b):
        local = _dot(mask_b, upper)
        tot_b = _dot(mask_b, ones_l).astype(BF16)
        return local, _dot(lower_strict, tot_b)

    for e in range(E):
        xe = aff3_ref[0, e]
        te = thr[e:e + 1, :]
        gt = xe > te
        eq = xe == te
        eq_f = jnp.where(eq, 1.0, 0.0)
        loc_eq, off_eq = prefix(eq_f.astype(BF16))
        rank_excl = loc_eq + off_eq - eq_f
        sel = gt | (eq & (rank_excl < need[e:e + 1, :]))
        sel_b = _onehot(sel)
        loc = _dot(sel_b, upper)
        tot_r = _nt(jnp.ones((LANES, LANES), BF16), sel_b)
        tot_rb = tot_r.astype(BF16)
        ct_r = _dot(tot_rb, upper_c)
        g = _onehot(ct_r[0:1, :] <= s_c)
        off_b = _nt(g, tot_rb)
        j_b = _dot(g, ones_c)
        j_r = _nt(jnp.ones((8, nch), BF16), g)
        nxt = pltpu.roll(loc, nch - 1, 0)
        dif = jnp.where(crow < nch - 1, nxt - loc, 0.0).astype(BF16)
        lg = loc[0:1, :] + _dot(g, dif)
        ind = _onehot(lg <= s_b - off_b)
        loc_b = _dot(ind, ones_l)
        loc_r = _nt(jnp.ones((8, LANES), BF16), ind)
        idx_ref[0, e:e + 1, :] = (j_r[0:1, :] * float(LANES) + loc_r[0:1, :]).astype(I32)
        oh = _onehot(chunk_id == j_b)
        pick = lane_id == loc_b
        x_hi = xe.astype(BF16)
        r1 = xe - x_hi.astype(F32)
        x_mid = r1.astype(BF16)
        x_lo = (r1 - x_mid.astype(F32)).astype(BF16)
        ones8 = jnp.ones((8, LANES), BF16)
        gate = None
        for piece in (x_hi, x_mid, x_lo):
            rows = _dot(oh, piece)
            val = _nt(ones8, jnp.where(pick, rows, 0.0).astype(BF16))
            gate = val if gate is None else gate + val
        gate_ref[0, e:e + 1, :] = gate[0:1, :]


def _route(aff2, aff3, cap):
    B, E, S = aff2.shape
    nch = S // LANES
    return pl.pallas_call(
        functools.partial(_route_kernel, cap=cap),
        grid=(B,),
        in_specs=[
            pl.BlockSpec((1, E, S), lambda b: (b, 0, 0)),
            pl.BlockSpec((1, E, nch, LANES), lambda b: (b, 0, 0, 0)),
        ],
        out_specs=[
            pl.BlockSpec((1, E, cap), lambda b: (b, 0, 0)),
            pl.BlockSpec((1, E, cap), lambda b: (b, 0, 0)),
        ],
        out_shape=[
            jax.ShapeDtypeStruct((B, E, cap), I32),
            jax.ShapeDtypeStruct((B, E, cap), F32),
        ],
        compiler_params=_cparams(("arbitrary",)),
        name="route",
    )(aff2, aff3)


def _gather_kernel(idx_ref, h8_hbm, xe_ref, hbuf, xr, sem, *, cap, unroll):
    b = pl.program_id(0)
    e = pl.program_id(1)
    D = xe_ref.shape[3]

    @pl.when(e == 0)
    def _load_tokens():
        cp = pltpu.make_async_copy(h8_hbm.at[b], hbuf, sem)
        cp.start()
        cp.wait()

    def body(k, carry):
        base = k * unroll
        slabs = []
        for u in range(unroll):
            t = idx_ref[0, 0, base + u]
            slabs.append(hbuf[pl.ds(pl.multiple_of(t * 8, 8), 8), :])
        for u in range(unroll):
            xr[pl.ds(pl.multiple_of((base + u) * 8, 8), 8), :] = slabs[u]
        return carry

    lax.fori_loop(0, cap // unroll, body, 0)
    for j in range(D // LANES):
        xe_ref[0, 0, :, j * LANES:(j + 1) * LANES] = xr[pl.ds(j, cap, stride=8), :].astype(xe_ref.dtype)


def _gather_tokens(idx3, h8, B, S, D, cap, unroll=8):
    E = N_EXPERTS
    return pl.pallas_call(
        functools.partial(_gather_kernel, cap=cap, unroll=unroll),
        grid=(B, E),
        in_specs=[
            pl.BlockSpec((1, 1, cap), lambda b, e: (b * E + e, 0, 0),
                         memory_space=pltpu.MemorySpace.SMEM),
            pl.BlockSpec(memory_space=pl.ANY),
        ],
        out_specs=pl.BlockSpec((1, 1, cap, D), lambda b, e: (b, e, 0, 0)),
        out_shape=jax.ShapeDtypeStruct((B, E, cap, D), BF16),
        scratch_shapes=[
            pltpu.VMEM((S * 8, LANES), F32),
            pltpu.VMEM((cap * 8, LANES), F32),
            pltpu.SemaphoreType.DMA(()),
        ],
        compiler_params=_cparams(("arbitrary", "arbitrary")),
        name="gather_tokens",
    )(idx3, h8)


def _ffn_kernel(xe_ref, wg_ref, wu_ref, wd_ref, y_ref, wgs, wus, wds, *, sub, wrows):
    b = pl.program_id(1)
    f = pl.program_id(2)
    cap, D = xe_ref.shape[2], xe_ref.shape[3]
    fb = wg_ref.shape[2]

    @pl.when(b == 0)
    def _cast_weights():
        for r in range(D // wrows):
            wgs[f, r * wrows:(r + 1) * wrows, :] = wg_ref[0, r * wrows:(r + 1) * wrows, :].astype(BF16)
            wus[f, r * wrows:(r + 1) * wrows, :] = wu_ref[0, r * wrows:(r + 1) * wrows, :].astype(BF16)
        for r in range(fb // wrows):
            wds[f, r * wrows:(r + 1) * wrows, :] = wd_ref[0, r * wrows:(r + 1) * wrows, :].astype(BF16)

    def compute(accumulate):
        for st in range(cap // sub):
            xs = xe_ref[0, 0, st * sub:(st + 1) * sub, :]
            g = _dot(xs, wgs[f])
            u = _dot(xs, wus[f])
            hid = (g * _sigmoid(g)) * u
            y = _dot(hid.astype(BF16), wds[f])
            for j in range(D // LANES):
                rows = pl.ds(st * sub * 8 + j, sub, stride=8)
                yj = y[:, j * LANES:(j + 1) * LANES]
                y_ref[0, 0, rows, :] = y_ref[0, 0, rows, :] + yj if accumulate else yj

    pl.when(f == 0)(functools.partial(compute, False))
    pl.when(f > 0)(functools.partial(compute, True))


def _ffn(xe, wg, wu, wd, nf=2, sub=256, wrows=256):
    B, E, cap, D = xe.shape
    FF = wg.shape[2]
    fb = FF // nf

    def wchunk(b, f):
        return jnp.where(b == 0, f, nf - 1)

    return pl.pallas_call(
        functools.partial(_ffn_kernel, sub=sub, wrows=wrows),
        grid=(E, B, nf),
        in_specs=[
            pl.BlockSpec((1, 1, cap, D), lambda e, b, f: (b, e, 0, 0)),
            pl.BlockSpec((1, D, fb), lambda e, b, f: (e, 0, wchunk(b, f))),
            pl.BlockSpec((1, D, fb), lambda e, b, f: (e, 0, wchunk(b, f))),
            pl.BlockSpec((1, fb, D), lambda e, b, f: (e, wchunk(b, f), 0)),
        ],
        out_specs=pl.BlockSpec((1, 1, cap * 8, LANES), lambda e, b, f: (b, e, 0, 0)),
        out_shape=jax.ShapeDtypeStruct((B, E, cap * 8, LANES), F32),
        scratch_shapes=[
            pltpu.VMEM((nf, D, fb), BF16),
            pltpu.VMEM((nf, D, fb), BF16),
            pltpu.VMEM((nf, fb, D), BF16),
        ],
        compiler_params=_cparams(("arbitrary", "arbitrary", "arbitrary")),
        name="expert_ffn",
    )(xe, wg, wu, wd)


def _combine_kernel(idx_ref, gate_ref, y_ref, h8_hbm, g_ref, b_ref, o_ref, acc, sem,
                    *, alpha, ne, cap, unroll):
    b = pl.program_id(0)
    j = pl.program_id(1)
    sub, D = o_ref.shape[1], o_ref.shape[2]
    rows_blk = 2048

    @pl.when(j == 0)
    def _init():
        cp = pltpu.make_async_copy(h8_hbm.at[b], acc, sem)
        cp.start()
        cp.wait()

        def body(k, carry):
            r = pl.multiple_of(k * rows_blk, rows_blk)
            acc[pl.ds(r, rows_blk), :] = alpha * acc[pl.ds(r, rows_blk), :]
            return carry

        lax.fori_loop(0, acc.shape[0] // rows_blk, body, 0)

    @pl.when(j < ne)
    def _scatter():
        def body(k, carry):
            base = k * unroll
            dsts, vals = [], []
            for u in range(unroll):
                i = base + u
                dst = pl.multiple_of(idx_ref[0, 0, i] * 8, 8)
                src = pl.multiple_of(i * 8, 8)
                dsts.append(dst)
                vals.append(acc[pl.ds(dst, 8), :] + gate_ref[0, 0, i] * y_ref[0, 0, pl.ds(src, 8), :])
            for u in range(unroll):
                acc[pl.ds(dsts[u], 8), :] = vals[u]
            return carry

        lax.fori_loop(0, cap // unroll, body, 0)

    @pl.when(j >= ne)
    def _finish():
        base = pl.multiple_of((j - ne) * (sub * 8), sub * 8)
        blk = acc.at[pl.ds(base, sub * 8)]
        parts = [blk[pl.ds(c, sub, stride=8), :] for c in range(D // LANES)]
        x = jnp.concatenate(parts, axis=1)
        o_ref[0] = _layer_norm(x, g_ref[...], b_ref[...])


def _combine(idx3, gate3, y, h8, g2, b2, B, S, D, cap, alpha, sub=512, unroll=8):
    E = N_EXPERTS
    nst = S // sub
    last = E - 1
    return pl.pallas_call(
        functools.partial(_combine_kernel, alpha=alpha, ne=E, cap=cap, unroll=unroll),
        grid=(B, E + nst),
        in_specs=[
            pl.BlockSpec((1, 1, cap), lambda b, j: (b * E + jnp.minimum(j, last), 0, 0),
                         memory_space=pltpu.MemorySpace.SMEM),
            pl.BlockSpec((1, 1, cap), lambda b, j: (b * E + jnp.minimum(j, last), 0, 0),
                         memory_space=pltpu.MemorySpace.SMEM),
            pl.BlockSpec((1, 1, cap * 8, LANES), lambda b, j: (b, jnp.minimum(j, last), 0, 0)),
            pl.BlockSpec(memory_space=pl.ANY),
            pl.BlockSpec((1, D), lambda b, j: (0, 0)),
            pl.BlockSpec((1, D), lambda b, j: (0, 0)),
        ],
        out_specs=pl.BlockSpec((1, sub, D), lambda b, j: (b, jnp.maximum(j - E, 0), 0)),
        out_shape=jax.ShapeDtypeStruct((B, S, D), F32),
        scratch_shapes=[pltpu.VMEM((S * 8, LANES), F32), pltpu.SemaphoreType.DMA(())],
        compiler_params=_cparams(("arbitrary", "arbitrary")),
        name="combine_ln",
    )(idx3, gate3, y, h8, g2, b2)


def _layer(x, w_in, b_in, rpb, sink, w_branch_a, w_branch_b, w_out, ln1_g, ln1_b,
           w_router, w_gate, w_up, w_down, ln2_g, ln2_b, alpha):
    B, S, D = x.shape
    T = B * S
    na_w = NA_HEADS * HEAD_DIM
    wq_w = WG_HEADS * HEAD_DIM
    kv_w = WG_KV_HEADS * HEAD_DIM
    o = np.cumsum([0, na_w, na_w, na_w, wq_w, kv_w, kv_w, D, D])
    scale = HEAD_DIM ** -0.5
    perm = np.concatenate([np.arange(h * HEAD_DIM, (h + 1) * HEAD_DIM) for h in WG_HEAD_ORDER])

    def cols(a):
        qa = a[..., o[0]:o[1]] * scale
        qb = (a[..., o[3]:o[4]] * scale)[..., perm]
        return jnp.concatenate([qa, a[..., o[1]:o[3]], qb, a[..., o[4]:o[6]]], axis=-1)

    w_qkv = cols(w_in).astype(BF16)
    b_qkv = cols(b_in)[None, :]
    w_g = w_in[:, o[6]:].astype(BF16)
    b_g = b_in[None, o[6]:]

    x2 = x.reshape(T, D)
    qkv_a, qkv_b = _qkv_proj(x2, w_qkv, b_qkv, 3 * na_w)
    ya = _natten(qkv_a.reshape(B, S, 3 * na_w), _na_bias_tables(rpb, S // GRID_W), B, S)
    yb = _wgqa(qkv_b.reshape(B, S, wq_w + 2 * kv_w), _wg_bias_tables(S), sink, B, S)

    wr_t = w_router.T
    wr_hi = wr_t.astype(BF16)
    wr_lo = (wr_t - wr_hi.astype(F32)).astype(BF16)
    h8, aff2, aff3 = _merge(
        x2, ya.reshape(T, na_w), yb.reshape(T, wq_w), w_g, b_g,
        w_branch_a.astype(BF16), w_branch_b[perm].astype(BF16), w_out.astype(BF16),
        ln1_g[None, :], ln1_b[None, :], wr_hi, wr_lo, B, S, alpha)

    cap = EC_CAPACITY_FACTOR * S // N_EXPERTS
    idx, gate = _route(aff2, aff3, cap)
    idx3 = idx.reshape(B * N_EXPERTS, 1, cap)
    gate3 = gate.reshape(B * N_EXPERTS, 1, cap)
    xe = _gather_tokens(idx3, h8.reshape(B, S * 8, LANES), B, S, D, cap)
    y = _ffn(xe, w_gate, w_up, w_down)
    return _combine(idx3, gate3, y, h8.reshape(B, S * 8, LANES),
                    ln2_g[None, :], ln2_b[None, :], B, S, D, cap, alpha)


def kernel(x, w_in, b_in, rpb, sink, w_branch_a, w_branch_b, w_out, ln1_g, ln1_b,
           w_router, w_gate, w_up, w_down, ln2_g, ln2_b):
    depth = w_in.shape[0]
    alpha = (2 * depth) ** 0.25
    for l in range(depth):
        x = _layer(x, w_in[l], b_in[l], rpb[l], sink[l], w_branch_a[l], w_branch_b[l],
                   w_out[l], ln1_g[l], ln1_b[l], w_router[l], w_gate[l], w_up[l],
                   w_down[l], ln2_g[l], ln2_b[l], alpha)
    return x
```

```python
import functools
import math

import jax
import jax.numpy as jnp
import numpy as np
from jax import lax
from jax.experimental import pallas as pl
from jax.experimental.pallas import tpu as pltpu

F32 = jnp.float32
BF16 = jnp.bfloat16
I32 = jnp.int32

LANES = 128
HEAD_DIM = 64
NA_HEADS = 8
NA_KR = 8
NA_KC = 16
GRID_W = 64
NA_GROWS = 4
NA_GROUPS = 4
NA_KCHUNKS = 3
WG_HEADS = 8
WG_KV_HEADS = 2
WINDOW = 128
WG_QBLK = 256
WG_KBLK = 512
WG_GROUPS = 2
N_EXPERTS = 16
EC_CAPACITY_FACTOR = 2
LN_EPS = 1e-5
NEG_INF = -1e30
MIN_NORMAL_BITS = 0x00800000
VMEM_LIMIT = 56 * 1024 * 1024


def _cparams(sem, vmem=VMEM_LIMIT):
    return pltpu.CompilerParams(dimension_semantics=sem, vmem_limit_bytes=vmem)


def _nt(a, b):
    return lax.dot_general(a, b, (((1,), (1,)), ((), ())), preferred_element_type=F32)


def _dot(a, b):
    return jnp.dot(a, b, preferred_element_type=F32)


def _qkv_kernel(x_ref, w_ref, b_ref, oa_ref, ob_ref, *, na_cols):
    x = x_ref[...].astype(BF16)
    acc = _dot(x, w_ref[...]) + b_ref[...]
    oa_ref[...] = acc[:, :na_cols].astype(BF16)
    ob_ref[...] = acc[:, na_cols:].astype(BF16)


def _qkv_proj(x2, w, b, na_cols, tm=1024):
    T, D = x2.shape
    N = w.shape[1]
    nb_cols = N - na_cols
    return pl.pallas_call(
        functools.partial(_qkv_kernel, na_cols=na_cols),
        grid=(T // tm,),
        in_specs=[
            pl.BlockSpec((tm, D), lambda i: (i, 0)),
            pl.BlockSpec((D, N), lambda i: (0, 0)),
            pl.BlockSpec((1, N), lambda i: (0, 0)),
        ],
        out_specs=[
            pl.BlockSpec((tm, na_cols), lambda i: (i, 0)),
            pl.BlockSpec((tm, nb_cols), lambda i: (i, 0)),
        ],
        out_shape=[
            jax.ShapeDtypeStruct((T, na_cols), BF16),
            jax.ShapeDtypeStruct((T, nb_cols), BF16),
        ],
        compiler_params=_cparams(("arbitrary",)),
        name="qkv_proj",
    )(x2, w, b)


def _na_bias_tables(rpb, rows):
    ngrp = rows // NA_GROWS
    nkr = NA_KCHUNKS * NA_GROWS
    rq = np.arange(NA_GROWS)
    kr = np.arange(nkr)
    dr = np.zeros((3, NA_GROWS, nkr), np.int32)
    vr = np.zeros((3, NA_GROWS, nkr), bool)
    for cls, gi in enumerate((0, 1, ngrp - 1)):
        kstart = int(np.clip(NA_GROWS * gi - NA_KR // 2, 0, rows - nkr))
        r = NA_GROWS * gi + rq
        r0 = np.clip(r - NA_KR // 2, 0, rows - NA_KR)
        krow = kstart + kr
        vr[cls] = (krow[None, :] >= r0[:, None]) & (krow[None, :] < r0[:, None] + NA_KR)
        dr[cls] = np.clip(krow[None, :] - r[:, None] + (NA_KR - 1), 0, 2 * NA_KR - 2)
    c = np.arange(GRID_W)
    c0 = np.clip(c - NA_KC // 2, 0, GRID_W - NA_KC)
    j = np.arange(GRID_W)
    vc = (j[None, :] >= c0[:, None]) & (j[None, :] < c0[:, None] + NA_KC)
    dc = np.clip(j[None, :] - c[:, None] + (NA_KC - 1), 0, 2 * NA_KC - 2)
    tmat = jnp.where(jnp.asarray(vc)[None, None], rpb[:, :, jnp.asarray(dc)], NEG_INF)
    full = tmat[:, jnp.asarray(dr)]
    full = jnp.where(jnp.asarray(vr)[None, :, :, :, None, None], full, NEG_INF)
    full = full.transpose(0, 1, 2, 4, 3, 5)
    H = rpb.shape[0]
    full = full.reshape(H // 2, 2, 3, NA_GROWS * GRID_W, nkr * GRID_W)
    return full.transpose(0, 2, 1, 3, 4).astype(F32)


def _na_kernel(q_ref, k_ref, v_ref, *refs, ngrp):
    rb = pl.program_id(2)
    bias_refs, o_ref = refs[:NA_GROUPS], refs[NA_GROUPS]
    gtok = NA_GROWS * GRID_W
    lane = lax.broadcasted_iota(I32, (1, LANES), 1)
    for g in range(NA_GROUPS):
        q = q_ref[0, g * gtok:(g + 1) * gtok, :]
        kstart = jnp.clip(NA_GROUPS * rb + g - 1, 0, ngrp - NA_KCHUNKS) * gtok
        out = jnp.zeros(q.shape, F32)
        ks, vs = [], []
        for i in range(NA_KCHUNKS):
            rows = pl.ds(pl.multiple_of(kstart + i * gtok, gtok), gtok)
            ks.append(k_ref[0, rows, :])
            vs.append(v_ref[0, rows, :])
        for hh in range(2):
            hm = (lane >= hh * HEAD_DIM) & (lane < (hh + 1) * HEAD_DIM)
            qh = jnp.where(hm, q, jnp.zeros_like(q))
            s = [_nt(qh, ks[i]) + bias_refs[g][0, 0, hh, :, i * gtok:(i + 1) * gtok]
                 for i in range(NA_KCHUNKS)]
            m = jnp.max(s[0], axis=1, keepdims=True)
            for i in range(1, NA_KCHUNKS):
                m = jnp.maximum(m, jnp.max(s[i], axis=1, keepdims=True))
            l = jnp.zeros_like(m)
            o = jnp.zeros(q.shape, F32)
            for i in range(NA_KCHUNKS):
                e = jnp.exp(s[i] - m)
                l = l + jnp.sum(e, axis=1, keepdims=True)
                o = o + _dot(e.astype(BF16), jnp.where(hm, vs[i], jnp.zeros_like(vs[i])))
            out = out + o * (1.0 / l)
        o_ref[0, g * gtok:(g + 1) * gtok, :] = out.astype(o_ref.dtype)


def _natten(qkv_a, bias_tab, B, S):
    rows = S // GRID_W
    ngrp = rows // NA_GROWS
    nblk = ngrp // NA_GROUPS
    gtok = NA_GROWS * GRID_W
    npair = NA_HEADS // 2

    def cls(rb, g):
        gi = NA_GROUPS * rb + g
        return jnp.where(gi == 0, 0, jnp.where(gi == ngrp - 1, 2, 1))

    in_specs = [
        pl.BlockSpec((1, NA_GROUPS * gtok, LANES), lambda h, b, rb: (b, rb, h)),
        pl.BlockSpec((1, S, LANES), lambda h, b, rb: (b, 0, npair + h)),
        pl.BlockSpec((1, S, LANES), lambda h, b, rb: (b, 0, 2 * npair + h)),
    ]
    for g in range(NA_GROUPS):
        in_specs.append(pl.BlockSpec(
            (1, 1, 2, gtok, NA_KCHUNKS * gtok), lambda h, b, rb, g=g: (h, cls(rb, g), 0, 0, 0)))
    return pl.pallas_call(
        functools.partial(_na_kernel, ngrp=ngrp),
        grid=(npair, B, nblk),
        in_specs=in_specs,
        out_specs=pl.BlockSpec((1, NA_GROUPS * gtok, LANES), lambda h, b, rb: (b, rb, h)),
        out_shape=jax.ShapeDtypeStruct((B, S, NA_HEADS * HEAD_DIM), BF16),
        compiler_params=_cparams(("arbitrary", "arbitrary", "arbitrary")),
        name="natten",
    )(qkv_a, qkv_a, qkv_a, *([bias_tab] * NA_GROUPS))


WG_HEAD_ORDER = (0, 4, 1, 5, 2, 6, 3, 7)


def _wg_bias_tables(S):
    q = np.arange(WG_QBLK)[:, None]
    k = np.arange(WG_KBLK)[None, :]
    tabs = []
    for delta in (0, WINDOW, WG_KBLK - WG_QBLK):
        dist = np.abs(q - k + delta).astype(np.float32)
        tabs.append(dist)
    dist = jnp.asarray(np.stack(tabs))
    slopes = jnp.exp2(-8.0 * (jnp.arange(WG_HEADS, dtype=F32) + 1.0) / WG_HEADS)
    bias = -slopes[None, :, None, None] * dist[:, None]
    return jnp.where(dist[:, None] <= WINDOW, bias, NEG_INF).astype(F32)


def _wg_kernel(sink_ref, q_ref, k_ref, v_ref, *refs, S):
    bias_refs, o_ref = refs[:WG_GROUPS], refs[WG_GROUPS]
    i = pl.program_id(1)
    lane = lax.broadcasted_iota(I32, (1, LANES), 1)
    for gq in range(WG_GROUPS):
        ig = i * WG_GROUPS + gq
        kstart = pl.multiple_of(jnp.clip(ig * WG_QBLK - WINDOW, 0, S - WG_KBLK), LANES)
        kw = k_ref[0, pl.ds(kstart, WG_KBLK), :]
        vw = v_ref[0, pl.ds(kstart, WG_KBLK), :]
        rows = slice(gq * WG_QBLK, (gq + 1) * WG_QBLK)
        for p in range(WG_HEADS // 2):
            q2 = q_ref[0, rows, p * LANES:(p + 1) * LANES]
            out = jnp.zeros(q2.shape, F32)
            for hh in range(2):
                head = WG_HEAD_ORDER[2 * p + hh]
                hm = (lane >= hh * HEAD_DIM) & (lane < (hh + 1) * HEAD_DIM)
                qh = jnp.where(hm, q2, jnp.zeros_like(q2))
                s = _nt(qh, kw) + bias_refs[gq][0, head]
                snk = sink_ref[head]
                m = jnp.maximum(jnp.max(s, axis=1, keepdims=True), snk)
                e = jnp.exp(s - m)
                l = jnp.sum(e, axis=1, keepdims=True) + jnp.exp(snk - m)
                o = _dot(e.astype(BF16), jnp.where(hm, vw, jnp.zeros_like(vw)))
                out = out + o * (1.0 / l)
            o_ref[0, rows, p * LANES:(p + 1) * LANES] = out.astype(o_ref.dtype)


def _wgqa(qkv_b, bias_tab, sink, B, S):
    nq = S // WG_QBLK
    qw = WG_HEADS * HEAD_DIM
    qblocks = qw // LANES
    qtok = WG_GROUPS * WG_QBLK

    def cls(i, gq):
        ig = i * WG_GROUPS + gq
        return jnp.where(ig == 0, 0, jnp.where(ig == nq - 1, 2, 1))

    in_specs = [
        pl.BlockSpec((1, qtok, qw), lambda b, i, s: (b, i, 0)),
        pl.BlockSpec((1, S, LANES), lambda b, i, s: (b, 0, qblocks)),
        pl.BlockSpec((1, S, LANES), lambda b, i, s: (b, 0, qblocks + 1)),
    ]
    for gq in range(WG_GROUPS):
        in_specs.append(pl.BlockSpec((1, WG_HEADS, WG_QBLK, WG_KBLK),
                                     lambda b, i, s, gq=gq: (cls(i, gq), 0, 0, 0)))
    return pl.pallas_call(
        functools.partial(_wg_kernel, S=S),
        grid_spec=pltpu.PrefetchScalarGridSpec(
            num_scalar_prefetch=1,
            grid=(B, nq // WG_GROUPS),
            in_specs=in_specs,
            out_specs=pl.BlockSpec((1, qtok, qw), lambda b, i, s: (b, i, 0)),
        ),
        out_shape=jax.ShapeDtypeStruct((B, S, qw), BF16),
        compiler_params=_cparams(("arbitrary", "arbitrary")),
        name="wgqa",
    )(sink, qkv_b, qkv_b, qkv_b, *([bias_tab] * WG_GROUPS))


def _sigmoid(x):
    return 1.0 / (1.0 + jnp.exp(-x))


def _layer_norm(x, g, b):
    mu = jnp.mean(x, axis=-1, keepdims=True)
    xc = x - mu
    var = jnp.mean(xc * xc, axis=-1, keepdims=True)
    return xc * lax.rsqrt(var + LN_EPS) * g + b


def _merge_kernel(x_ref, ya_ref, yb_ref, wg_ref, bg_ref, wa_ref, wb_ref, wo_ref,
                  g_ref, b_ref, wrh_ref, wrl_ref,
                  h8_ref, aff2_ref, aff3_ref, *, alpha, sub):
    tm, D = x_ref.shape
    for st in range(tm // sub):
        r0 = st * sub
        x = x_ref[r0:r0 + sub, :]
        gates = _dot(x.astype(BF16), wg_ref[...]) + bg_ref[...]
        a = _dot(ya_ref[r0:r0 + sub, :], wa_ref[...])
        bm = _dot(yb_ref[r0:r0 + sub, :], wb_ref[...])
        mix = _sigmoid(gates[:, :D]) * a + _sigmoid(gates[:, D:]) * bm
        o = _dot(mix.astype(BF16), wo_ref[...])
        h = _layer_norm(alpha * x + o, g_ref[...], b_ref[...])
        for j in range(D // LANES):
            h8_ref[pl.ds(r0 * 8 + j, sub, stride=8), :] = h[:, j * LANES:(j + 1) * LANES]
        h_hi = h.astype(BF16)
        h_lo = (h - h_hi.astype(F32)).astype(BF16)
        lg = _nt(wrh_ref[...], h_hi) + (_nt(wrl_ref[...], h_hi) + _nt(wrh_ref[...], h_lo))
        mx = jnp.max(lg, axis=0, keepdims=True)
        ex = jnp.exp(lg - mx)
        aff = ex / jnp.sum(ex, axis=0, keepdims=True)
        aff2_ref[0, :, r0:r0 + sub] = aff
        for c in range(sub // LANES):
            cc = r0 // LANES + c
            for e in range(N_EXPERTS):
                aff3_ref[0, e, cc:cc + 1, :] = aff[e:e + 1, c * LANES:(c + 1) * LANES]


def _merge(x2, ya, yb, wg, bg, wa, wb, wo, g1, b1, wr_hi, wr_lo, B, S, alpha, tm=1024, sub=1024):
    T, D = x2.shape
    nt = S // tm
    E = N_EXPERTS
    const = lambda i: (0, 0)
    return pl.pallas_call(
        functools.partial(_merge_kernel, alpha=alpha, sub=sub),
        grid=(T // tm,),
        in_specs=[
            pl.BlockSpec((tm, D), lambda i: (i, 0)),
            pl.BlockSpec((tm, ya.shape[1]), lambda i: (i, 0)),
            pl.BlockSpec((tm, yb.shape[1]), lambda i: (i, 0)),
            pl.BlockSpec(wg.shape, const),
            pl.BlockSpec(bg.shape, const),
            pl.BlockSpec(wa.shape, const),
            pl.BlockSpec(wb.shape, const),
            pl.BlockSpec(wo.shape, const),
            pl.BlockSpec(g1.shape, const),
            pl.BlockSpec(b1.shape, const),
            pl.BlockSpec(wr_hi.shape, const),
            pl.BlockSpec(wr_lo.shape, const),
        ],
        out_specs=[
            pl.BlockSpec((tm * 8, LANES), lambda i: (i, 0)),
            pl.BlockSpec((1, E, tm), lambda i: (i // nt, 0, i % nt)),
            pl.BlockSpec((1, E, tm // LANES, LANES), lambda i: (i // nt, 0, i % nt, 0)),
        ],
        out_shape=[
            jax.ShapeDtypeStruct((T * 8, LANES), F32),
            jax.ShapeDtypeStruct((B, E, S), F32),
            jax.ShapeDtypeStruct((B, E, S // LANES, LANES), F32),
        ],
        compiler_params=_cparams(("arbitrary",)),
        name="merge_ln_router",
    )(x2, ya, yb, wg, bg, wa, wb, wo, g1, b1, wr_hi, wr_lo)


def _onehot(cond):
    return jnp.where(cond, 1.0, 0.0).astype(BF16)


def _route_kernel(aff2_ref, aff3_ref, idx_ref, gate_ref, *, cap):
    E, nch = aff3_ref.shape[1], aff3_ref.shape[2]
    capf = float(cap)
    a2 = aff2_ref[0]

    def bit_step(it, prefix):
        cand = prefix | jnp.left_shift(jnp.int32(1), 30 - it)
        cnt = jnp.sum(jnp.where(a2 >= lax.bitcast_convert_type(cand, F32), 1.0, 0.0),
                      axis=1, keepdims=True)
        return jnp.where((cnt >= capf) & (cand >= MIN_NORMAL_BITS), cand, prefix)

    thr = lax.bitcast_convert_type(lax.fori_loop(0, 31, bit_step, jnp.zeros((E, 1), I32)), F32)
    n_gt = jnp.sum(jnp.where(a2 > thr, 1.0, 0.0), axis=1, keepdims=True)
    need = capf - n_gt

    ri = lax.broadcasted_iota(I32, (LANES, LANES), 0)
    ci = lax.broadcasted_iota(I32, (LANES, LANES), 1)
    upper = _onehot(ri <= ci)
    ones_l = jnp.ones((LANES, LANES), BF16)
    rc = lax.broadcasted_iota(I32, (nch, nch), 0)
    cc = lax.broadcasted_iota(I32, (nch, nch), 1)
    lower_strict = _onehot(cc < rc)
    upper_c = _onehot(rc <= cc)
    ones_c = jnp.ones((nch, nch), BF16)
    crow = lax.broadcasted_iota(I32, (nch, LANES), 0)
    s_b = lax.broadcasted_iota(I32, (cap, LANES), 0).astype(F32)
    s_c = lax.broadcasted_iota(I32, (cap, nch), 0).astype(F32)
    chunk_id = lax.broadcasted_iota(I32, (cap, nch), 1).astype(F32)
    lane_id = lax.broadcasted_iota(I32, (cap, LANES), 1).astype(F32)

    def prefix(mask_b):
        local = _dot(mask_b, upper)
        tot_b = _dot(mask_b, ones_l).astype(BF16)
        return local, _dot(lower_strict, tot_b)

    for e in range(E):
        xe = aff3_ref[0, e]
        te = thr[e:e + 1, :]
        gt = xe > te
        eq = xe == te
        eq_f = jnp.where(eq, 1.0, 0.0)
        loc_eq, off_eq = prefix(eq_f.astype(BF16))
        rank_excl = loc_eq + off_eq - eq_f
        sel = gt | (eq & (rank_excl < need[e:e + 1, :]))
        sel_b = _onehot(sel)
        loc = _dot(sel_b, upper)
        tot_r = _nt(jnp.ones((LANES, LANES), BF16), sel_b)
        tot_rb = tot_r.astype(BF16)
        ct_r = _dot(tot_rb, upper_c)
        g = _onehot(ct_r[0:1, :] <= s_c)
        off_b = _nt(g, tot_rb)
        j_b = _dot(g, ones_c)
        j_r = _nt(jnp.ones((8, nch), BF16), g)
        nxt = pltpu.roll(loc, nch - 1, 0)
        dif = jnp.where(crow < nch - 1, nxt - loc, 0.0).astype(BF16)
        lg = loc[0:1, :] + _dot(g, dif)
        ind = _onehot(lg <= s_b - off_b)
        loc_b = _dot(ind, ones_l)
        loc_r = _nt(jnp.ones((8, LANES), BF16), ind)
        idx_ref[0, e:e + 1, :] = (j_r[0:1, :] * float(LANES) + loc_r[0:1, :]).astype(I32)
        oh = _onehot(chunk_id == j_b)
        pick = lane_id == loc_b
        x_hi = xe.astype(BF16)
        r1 = xe - x_hi.astype(F32)
        x_mid = r1.astype(BF16)
        x_lo = (r1 - x_mid.astype(F32)).astype(BF16)
        ones8 = jnp.ones((8, LANES), BF16)
        gate = None
        for piece in (x_hi, x_mid, x_lo):
            rows = _dot(oh, piece)
            val = _nt(ones8, jnp.where(pick, rows, 0.0).astype(BF16))
            gate = val if gate is None else gate + val
        gate_ref[0, e:e + 1, :] = gate[0:1, :]


def _route(aff2, aff3, cap):
    B, E, S = aff2.shape
    nch = S // LANES
    return pl.pallas_call(
        functools.partial(_route_kernel, cap=cap),
        grid=(B,),
        in_specs=[
            pl.BlockSpec((1, E, S), lambda b: (b, 0, 0)),
            pl.BlockSpec((1, E, nch, LANES), lambda b: (b, 0, 0, 0)),
        ],
        out_specs=[
            pl.BlockSpec((1, E, cap), lambda b: (b, 0, 0)),
            pl.BlockSpec((1, E, cap), lambda b: (b, 0, 0)),
        ],
        out_shape=[
            jax.ShapeDtypeStruct((B, E, cap), I32),
            jax.ShapeDtypeStruct((B, E, cap), F32),
        ],
        compiler_params=_cparams(("arbitrary",)),
        name="route",
    )(aff2, aff3)


def _gather_kernel(idx_ref, h8_hbm, xe_ref, hbuf, xr, sem, *, cap, unroll):
    b = pl.program_id(0)
    e = pl.program_id(1)
    D = xe_ref.shape[3]

    @pl.when(e == 0)
    def _load_tokens():
        cp = pltpu.make_async_copy(h8_hbm.at[b], hbuf, sem)
        cp.start()
        cp.wait()

    def body(k, carry):
        base = k * unroll
        slabs = []
        for u in range(unroll):
            t = idx_ref[0, 0, base + u]
            slabs.append(hbuf[pl.ds(pl.multiple_of(t * 8, 8), 8), :])
        for u in range(unroll):
            xr[pl.ds(pl.multiple_of((base + u) * 8, 8), 8), :] = slabs[u]
        return carry

    lax.fori_loop(0, cap // unroll, body, 0)
    for j in range(D // LANES):
        xe_ref[0, 0, :, j * LANES:(j + 1) * LANES] = xr[pl.ds(j, cap, stride=8), :].astype(xe_ref.dtype)


def _gather_tokens(idx3, h8, B, S, D, cap, unroll=8):
    E = N_EXPERTS
    return pl.pallas_call(
        functools.partial(_gather_kernel, cap=cap, unroll=unroll),
        grid=(B, E),
        in_specs=[
            pl.BlockSpec((1, 1, cap), lambda b, e: (b * E + e, 0, 0),
                         memory_space=pltpu.MemorySpace.SMEM),
            pl.BlockSpec(memory_space=pl.ANY),
        ],
        out_specs=pl.BlockSpec((1, 1, cap, D), lambda b, e: (b, e, 0, 0)),
        out_shape=jax.ShapeDtypeStruct((B, E, cap, D), BF16),
        scratch_shapes=[
            pltpu.VMEM((S * 8, LANES), F32),
            pltpu.VMEM((cap * 8, LANES), F32),
            pltpu.SemaphoreType.DMA(()),
        ],
        compiler_params=_cparams(("arbitrary", "arbitrary")),
        name="gather_tokens",
    )(idx3, h8)


def _ffn_kernel(xe_ref, wg_ref, wu_ref, wd_ref, y_ref, wgs, wus, wds, *, sub, wrows):
    b = pl.program_id(1)
    f = pl.program_id(2)
    cap, D = xe_ref.shape[2], xe_ref.shape[3]
    fb = wg_ref.shape[2]

    @pl.when(b == 0)
    def _cast_weights():
        for r in range(D // wrows):
            wgs[f, r * wrows:(r + 1) * wrows, :] = wg_ref[0, r * wrows:(r + 1) * wrows, :].astype(BF16)
            wus[f, r * wrows:(r + 1) * wrows, :] = wu_ref[0, r * wrows:(r + 1) * wrows, :].astype(BF16)
        for r in range(fb // wrows):
            wds[f, r * wrows:(r + 1) * wrows, :] = wd_ref[0, r * wrows:(r + 1) * wrows, :].astype(BF16)

    def compute(accumulate):
        for st in range(cap // sub):
            xs = xe_ref[0, 0, st * sub:(st + 1) * sub, :]
            g = _dot(xs, wgs[f])
            u = _dot(xs, wus[f])
            hid = (g * _sigmoid(g)) * u
            y = _dot(hid.astype(BF16), wds[f])
            for j in range(D // LANES):
                rows = pl.ds(st * sub * 8 + j, sub, stride=8)
                yj = y[:, j * LANES:(j + 1) * LANES]
                y_ref[0, 0, rows, :] = y_ref[0, 0, rows, :] + yj if accumulate else yj

    pl.when(f == 0)(functools.partial(compute, False))
    pl.when(f > 0)(functools.partial(compute, True))


def _ffn(xe, wg, wu, wd, nf=2, sub=256, wrows=256):
    B, E, cap, D = xe.shape
    FF = wg.shape[2]
    fb = FF // nf

    def wchunk(b, f):
        return jnp.where(b == 0, f, nf - 1)

    return pl.pallas_call(
        functools.partial(_ffn_kernel, sub=sub, wrows=wrows),
        grid=(E, B, nf),
        in_specs=[
            pl.BlockSpec((1, 1, cap, D), lambda e, b, f: (b, e, 0, 0)),
            pl.BlockSpec((1, D, fb), lambda e, b, f: (e, 0, wchunk(b, f))),
            pl.BlockSpec((1, D, fb), lambda e, b, f: (e, 0, wchunk(b, f))),
            pl.BlockSpec((1, fb, D), lambda e, b, f: (e, wchunk(b, f), 0)),
        ],
        out_specs=pl.BlockSpec((1, 1, cap * 8, LANES), lambda e, b, f: (b, e, 0, 0)),
        out_shape=jax.ShapeDtypeStruct((B, E, cap * 8, LANES), F32),
        scratch_shapes=[
            pltpu.VMEM((nf, D, fb), BF16),
            pltpu.VMEM((nf, D, fb), BF16),
            pltpu.VMEM((nf, fb, D), BF16),
        ],
        compiler_params=_cparams(("arbitrary", "arbitrary", "arbitrary")),
        name="expert_ffn",
    )(xe, wg, wu, wd)


def _combine_kernel(idx_ref, gate_ref, y_ref, h8_hbm, g_ref, b_ref, o_ref, acc, sem,
                    *, alpha, ne, cap, unroll):
    b = pl.program_id(0)
    j = pl.program_id(1)
    sub, D = o_ref.shape[1], o_ref.shape[2]
    rows_blk = 2048

    @pl.when(j == 0)
    def _init():
        cp = pltpu.make_async_copy(h8_hbm.at[b], acc, sem)
        cp.start()
        cp.wait()

        def body(k, carry):
            r = pl.multiple_of(k * rows_blk, rows_blk)
            acc[pl.ds(r, rows_blk), :] = alpha * acc[pl.ds(r, rows_blk), :]
            return carry

        lax.fori_loop(0, acc.shape[0] // rows_blk, body, 0)

    @pl.when(j < ne)
    def _scatter():
        def body(k, carry):
            base = k * unroll
            dsts, vals = [], []
            for u in range(unroll):
                i = base + u
                dst = pl.multiple_of(idx_ref[0, 0, i] * 8, 8)
                src = pl.multiple_of(i * 8, 8)
                dsts.append(dst)
                vals.append(acc[pl.ds(dst, 8), :] + gate_ref[0, 0, i] * y_ref[0, 0, pl.ds(src, 8), :])
            for u in range(unroll):
                acc[pl.ds(dsts[u], 8), :] = vals[u]
            return carry

        lax.fori_loop(0, cap // unroll, body, 0)

    @pl.when(j >= ne)
    def _finish():
        base = pl.multiple_of((j - ne) * (sub * 8), sub * 8)
        blk = acc.at[pl.ds(base, sub * 8)]
        parts = [blk[pl.ds(c, sub, stride=8), :] for c in range(D // LANES)]
        x = jnp.concatenate(parts, axis=1)
        o_ref[0] = _layer_norm(x, g_ref[...], b_ref[...])


def _combine(idx3, gate3, y, h8, g2, b2, B, S, D, cap, alpha, sub=512, unroll=8):
    E = N_EXPERTS
    nst = S // sub
    last = E - 1
    return pl.pallas_call(
        functools.partial(_combine_kernel, alpha=alpha, ne=E, cap=cap, unroll=unroll),
        grid=(B, E + nst),
        in_specs=[
            pl.BlockSpec((1, 1, cap), lambda b, j: (b * E + jnp.minimum(j, last), 0, 0),
                         memory_space=pltpu.MemorySpace.SMEM),
            pl.BlockSpec((1, 1, cap), lambda b, j: (b * E + jnp.minimum(j, last), 0, 0),
                         memory_space=pltpu.MemorySpace.SMEM),
            pl.BlockSpec((1, 1, cap * 8, LANES), lambda b, j: (b, jnp.minimum(j, last), 0, 0)),
            pl.BlockSpec(memory_space=pl.ANY),
            pl.BlockSpec((1, D), lambda b, j: (0, 0)),
            pl.BlockSpec((1, D), lambda b, j: (0, 0)),
        ],
        out_specs=pl.BlockSpec((1, sub, D), lambda b, j: (b, jnp.maximum(j - E, 0), 0)),
        out_shape=jax.ShapeDtypeStruct((B, S, D), F32),
        scratch_shapes=[pltpu.VMEM((S * 8, LANES), F32), pltpu.SemaphoreType.DMA(())],
        compiler_params=_cparams(("arbitrary", "arbitrary")),
        name="combine_ln",
    )(idx3, gate3, y, h8, g2, b2)


def _layer(x, w_in, b_in, rpb, sink, w_branch_a, w_branch_b, w_out, ln1_g, ln1_b,
           w_router, w_gate, w_up, w_down, ln2_g, ln2_b, alpha):
    B, S, D = x.shape
    T = B * S
    na_w = NA_HEADS * HEAD_DIM
    wq_w = WG_HEADS * HEAD_DIM
    kv_w = WG_KV_HEADS * HEAD_DIM
    o = np.cumsum([0, na_w, na_w, na_w, wq_w, kv_w, kv_w, D, D])
    scale = HEAD_DIM ** -0.5
    perm = np.concatenate([np.arange(h * HEAD_DIM, (h + 1) * HEAD_DIM) for h in WG_HEAD_ORDER])

    def cols(a):
        qa = a[..., o[0]:o[1]] * scale
        qb = (a[..., o[3]:o[4]] * scale)[..., perm]
        return jnp.concatenate([qa, a[..., o[1]:o[3]], qb, a[..., o[4]:o[6]]], axis=-1)

    w_qkv = cols(w_in).astype(BF16)
    b_qkv = cols(b_in)[None, :]
    w_g = w_in[:, o[6]:].astype(BF16)
    b_g = b_in[None, o[6]:]

    x2 = x.reshape(T, D)
    qkv_a, qkv_b = _qkv_proj(x2, w_qkv, b_qkv, 3 * na_w)
    ya = _natten(qkv_a.reshape(B, S, 3 * na_w), _na_bias_tables(rpb, S // GRID_W), B, S)
    yb = _wgqa(qkv_b.reshape(B, S, wq_w + 2 * kv_w), _wg_bias_tables(S), sink, B, S)

    wr_t = w_router.T
    wr_hi = wr_t.astype(BF16)
    wr_lo = (wr_t - wr_hi.astype(F32)).astype(BF16)
    h8, aff2, aff3 = _merge(
        x2, ya.reshape(T, na_w), yb.reshape(T, wq_w), w_g, b_g,
        w_branch_a.astype(BF16), w_branch_b[perm].astype(BF16), w_out.astype(BF16),
        ln1_g[None, :], ln1_b[None, :], wr_hi, wr_lo, B, S, alpha)

    cap = EC_CAPACITY_FACTOR * S // N_EXPERTS
    idx, gate = _route(aff2, aff3, cap)
    idx3 = idx.reshape(B * N_EXPERTS, 1, cap)
    gate3 = gate.reshape(B * N_EXPERTS, 1, cap)
    xe = _gather_tokens(idx3, h8.reshape(B, S * 8, LANES), B, S, D, cap)
    y = _ffn(xe, w_gate, w_up, w_down)
    return _combine(idx3, gate3, y, h8.reshape(B, S * 8, LANES),
                    ln2_g[None, :], ln2_b[None, :], B, S, D, cap, alpha)


def kernel(x, w_in, b_in, rpb, sink, w_branch_a, w_branch_b, w_out, ln1_g, ln1_b,
           w_router, w_gate, w_up, w_down, ln2_g, ln2_b):
    depth = w_in.shape[0]
    alpha = (2 * depth) ** 0.25
    for l in range(depth):
        x = _layer(x, w_in[l], b_in[l], rpb[l], sink[l], w_branch_a[l], w_branch_b[l],
                   w_out[l], ln1_g[l], ln1_b[l], w_router[l], w_gate[l], w_up[l],
                   w_down[l], ln2_g[l], ln2_b[l], alpha)
    return x
```

```python
import functools
import math

import jax
import jax.numpy as jnp
import numpy as np
from jax import lax
from jax.experimental import pallas as pl
from jax.experimental.pallas import tpu as pltpu

F32 = jnp.float32
BF16 = jnp.bfloat16
I32 = jnp.int32

LANES = 128
HEAD_DIM = 64
NA_HEADS = 8
NA_KR = 8
NA_KC = 16
GRID_W = 64
NA_GROWS = 4
NA_GROUPS = 8
NA_KCHUNKS = 3
WG_HEADS = 8
WG_KV_HEADS = 2
WINDOW = 128
WG_QBLK = 256
WG_KBLK = 512
WG_GROUPS = 2
N_EXPERTS = 16
EC_CAPACITY_FACTOR = 2
LN_EPS = 1e-5
NEG_INF = -1e30
MIN_NORMAL_BITS = 0x00800000
VMEM_LIMIT = 56 * 1024 * 1024


def _cparams(sem, vmem=VMEM_LIMIT):
    return pltpu.CompilerParams(dimension_semantics=sem, vmem_limit_bytes=vmem)


def _nt(a, b):
    return lax.dot_general(a, b, (((1,), (1,)), ((), ())), preferred_element_type=F32)


def _dot(a, b):
    return jnp.dot(a, b, preferred_element_type=F32)


def _qkv_kernel(x_ref, w_ref, b_ref, oa_ref, ob_ref, *, na_cols):
    x = x_ref[...].astype(BF16)
    acc = _dot(x, w_ref[...]) + b_ref[...]
    oa_ref[...] = acc[:, :na_cols].astype(BF16)
    ob_ref[...] = acc[:, na_cols:].astype(BF16)


def _qkv_proj(x2, w, b, na_cols, tm=1024):
    T, D = x2.shape
    N = w.shape[1]
    nb_cols = N - na_cols
    return pl.pallas_call(
        functools.partial(_qkv_kernel, na_cols=na_cols),
        grid=(T // tm,),
        in_specs=[
            pl.BlockSpec((tm, D), lambda i: (i, 0)),
            pl.BlockSpec((D, N), lambda i: (0, 0)),
            pl.BlockSpec((1, N), lambda i: (0, 0)),
        ],
        out_specs=[
            pl.BlockSpec((tm, na_cols), lambda i: (i, 0)),
            pl.BlockSpec((tm, nb_cols), lambda i: (i, 0)),
        ],
        out_shape=[
            jax.ShapeDtypeStruct((T, na_cols), BF16),
            jax.ShapeDtypeStruct((T, nb_cols), BF16),
        ],
        compiler_params=_cparams(("arbitrary",)),
        name="qkv_proj",
    )(x2, w, b)


def _na_bias_tables(rpb, rows):
    ngrp = rows // NA_GROWS
    nkr = NA_KCHUNKS * NA_GROWS
    rq = np.arange(NA_GROWS)
    kr = np.arange(nkr)
    dr = np.zeros((3, NA_GROWS, nkr), np.int32)
    vr = np.zeros((3, NA_GROWS, nkr), bool)
    for cls, gi in enumerate((0, 1, ngrp - 1)):
        kstart = int(np.clip(NA_GROWS * gi - NA_KR // 2, 0, rows - nkr))
        r = NA_GROWS * gi + rq
        r0 = np.clip(r - NA_KR // 2, 0, rows - NA_KR)
        krow = kstart + kr
        vr[cls] = (krow[None, :] >= r0[:, None]) & (krow[None, :] < r0[:, None] + NA_KR)
        dr[cls] = np.clip(krow[None, :] - r[:, None] + (NA_KR - 1), 0, 2 * NA_KR - 2)
    c = np.arange(GRID_W)
    c0 = np.clip(c - NA_KC // 2, 0, GRID_W - NA_KC)
    j = np.arange(GRID_W)
    vc = (j[None, :] >= c0[:, None]) & (j[None, :] < c0[:, None] + NA_KC)
    dc = np.clip(j[None, :] - c[:, None] + (NA_KC - 1), 0, 2 * NA_KC - 2)
    tmat = jnp.where(jnp.asarray(vc)[None, None], rpb[:, :, jnp.asarray(dc)], NEG_INF)
    full = tmat[:, jnp.asarray(dr)]
    full = jnp.where(jnp.asarray(vr)[None, :, :, :, None, None], full, NEG_INF)
    full = full.transpose(0, 1, 2, 4, 3, 5)
    H = rpb.shape[0]
    full = full.reshape(H // 2, 2, 3, NA_GROWS * GRID_W, nkr * GRID_W)
    return full.transpose(0, 2, 1, 3, 4).astype(F32)


def _na_kernel(q_ref, k_ref, v_ref, *refs, ngrp):
    rb = pl.program_id(2)
    bias_refs, o_ref = refs[:NA_GROUPS], refs[NA_GROUPS]
    gtok = NA_GROWS * GRID_W
    lane = lax.broadcasted_iota(I32, (1, LANES), 1)
    for g in range(NA_GROUPS):
        q = q_ref[0, g * gtok:(g + 1) * gtok, :]
        kstart = jnp.clip(NA_GROUPS * rb + g - 1, 0, ngrp - NA_KCHUNKS) * gtok
        out = jnp.zeros(q.shape, F32)
        ks, vs = [], []
        for i in range(NA_KCHUNKS):
            rows = pl.ds(pl.multiple_of(kstart + i * gtok, gtok), gtok)
            ks.append(k_ref[0, rows, :])
            vs.append(v_ref[0, rows, :])
        for hh in range(2):
            hm = (lane >= hh * HEAD_DIM) & (lane < (hh + 1) * HEAD_DIM)
            qh = jnp.where(hm, q, jnp.zeros_like(q))
            s = [_nt(qh, ks[i]) + bias_refs[g][0, 0, hh, :, i * gtok:(i + 1) * gtok]
                 for i in range(NA_KCHUNKS)]
            m = jnp.max(s[0], axis=1, keepdims=True)
            for i in range(1, NA_KCHUNKS):
                m = jnp.maximum(m, jnp.max(s[i], axis=1, keepdims=True))
            l = jnp.zeros_like(m)
            o = jnp.zeros(q.shape, F32)
            for i in range(NA_KCHUNKS):
                e = jnp.exp(s[i] - m)
                l = l + jnp.sum(e, axis=1, keepdims=True)
                o = o + _dot(e.astype(BF16), jnp.where(hm, vs[i], jnp.zeros_like(vs[i])))
            out = out + o * (1.0 / l)
        o_ref[0, g * gtok:(g + 1) * gtok, :] = out.astype(o_ref.dtype)


def _natten(qkv_a, bias_tab, B, S):
    rows = S // GRID_W
    ngrp = rows // NA_GROWS
    nblk = ngrp // NA_GROUPS
    gtok = NA_GROWS * GRID_W
    npair = NA_HEADS // 2

    def cls(rb, g):
        gi = NA_GROUPS * rb + g
        return jnp.where(gi == 0, 0, jnp.where(gi == ngrp - 1, 2, 1))

    in_specs = [
        pl.BlockSpec((1, NA_GROUPS * gtok, LANES), lambda h, b, rb: (b, rb, h)),
        pl.BlockSpec((1, S, LANES), lambda h, b, rb: (b, 0, npair + h)),
        pl.BlockSpec((1, S, LANES), lambda h, b, rb: (b, 0, 2 * npair + h)),
    ]
    for g in range(NA_GROUPS):
        in_specs.append(pl.BlockSpec(
            (1, 1, 2, gtok, NA_KCHUNKS * gtok), lambda h, b, rb, g=g: (h, cls(rb, g), 0, 0, 0)))
    return pl.pallas_call(
        functools.partial(_na_kernel, ngrp=ngrp),
        grid=(npair, B, nblk),
        in_specs=in_specs,
        out_specs=pl.BlockSpec((1, NA_GROUPS * gtok, LANES), lambda h, b, rb: (b, rb, h)),
        out_shape=jax.ShapeDtypeStruct((B, S, NA_HEADS * HEAD_DIM), BF16),
        compiler_params=_cparams(("arbitrary", "arbitrary", "arbitrary")),
        name="natten",
    )(qkv_a, qkv_a, qkv_a, *([bias_tab] * NA_GROUPS))


WG_HEAD_ORDER = (0, 4, 1, 5, 2, 6, 3, 7)


def _wg_bias_tables(S):
    q = np.arange(WG_QBLK)[:, None]
    k = np.arange(WG_KBLK)[None, :]
    tabs = []
    for delta in (0, WINDOW, WG_KBLK - WG_QBLK):
        dist = np.abs(q - k + delta).astype(np.float32)
        tabs.append(dist)
    dist = jnp.asarray(np.stack(tabs))
    slopes = jnp.exp2(-8.0 * (jnp.arange(WG_HEADS, dtype=F32) + 1.0) / WG_HEADS)
    bias = -slopes[None, :, None, None] * dist[:, None]
    return jnp.where(dist[:, None] <= WINDOW, bias, NEG_INF).astype(F32)


def _wg_kernel(sink_ref, q_ref, k_ref, v_ref, *refs, S):
    bias_refs, o_ref = refs[:WG_GROUPS], refs[WG_GROUPS]
    i = pl.program_id(1)
    lane = lax.broadcasted_iota(I32, (1, LANES), 1)
    for gq in range(WG_GROUPS):
        ig = i * WG_GROUPS + gq
        kstart = pl.multiple_of(jnp.clip(ig * WG_QBLK - WINDOW, 0, S - WG_KBLK), LANES)
        kw = k_ref[0, pl.ds(kstart, WG_KBLK), :]
        vw = v_ref[0, pl.ds(kstart, WG_KBLK), :]
        rows = slice(gq * WG_QBLK, (gq + 1) * WG_QBLK)
        for p in range(WG_HEADS // 2):
            q2 = q_ref[0, rows, p * LANES:(p + 1) * LANES]
            out = jnp.zeros(q2.shape, F32)
            for hh in range(2):
                head = WG_HEAD_ORDER[2 * p + hh]
                hm = (lane >= hh * HEAD_DIM) & (lane < (hh + 1) * HEAD_DIM)
                qh = jnp.where(hm, q2, jnp.zeros_like(q2))
                s = _nt(qh, kw) + bias_refs[gq][0, head]
                snk = sink_ref[head]
                m = jnp.maximum(jnp.max(s, axis=1, keepdims=True), snk)
                e = jnp.exp(s - m)
                l = jnp.sum(e, axis=1, keepdims=True) + jnp.exp(snk - m)
                o = _dot(e.astype(BF16), jnp.where(hm, vw, jnp.zeros_like(vw)))
                out = out + o * (1.0 / l)
            o_ref[0, rows, p * LANES:(p + 1) * LANES] = out.astype(o_ref.dtype)


def _wgqa(qkv_b, bias_tab, sink, B, S):
    nq = S // WG_QBLK
    qw = WG_HEADS * HEAD_DIM
    qblocks = qw // LANES
    qtok = WG_GROUPS * WG_QBLK

    def cls(i, gq):
        ig = i * WG_GROUPS + gq
        return jnp.where(ig == 0, 0, jnp.where(ig == nq - 1, 2, 1))

    in_specs = [
        pl.BlockSpec((1, qtok, qw), lambda b, i, s: (b, i, 0)),
        pl.BlockSpec((1, S, LANES), lambda b, i, s: (b, 0, qblocks)),
        pl.BlockSpec((1, S, LANES), lambda b, i, s: (b, 0, qblocks + 1)),
    ]
    for gq in range(WG_GROUPS):
        in_specs.append(pl.BlockSpec((1, WG_HEADS, WG_QBLK, WG_KBLK),
                                     lambda b, i, s, gq=gq: (cls(i, gq), 0, 0, 0)))
    return pl.pallas_call(
        functools.partial(_wg_kernel, S=S),
        grid_spec=pltpu.PrefetchScalarGridSpec(
            num_scalar_prefetch=1,
            grid=(B, nq // WG_GROUPS),
            in_specs=in_specs,
            out_specs=pl.BlockSpec((1, qtok, qw), lambda b, i, s: (b, i, 0)),
        ),
        out_shape=jax.ShapeDtypeStruct((B, S, qw), BF16),
        compiler_params=_cparams(("arbitrary", "arbitrary")),
        name="wgqa",
    )(sink, qkv_b, qkv_b, qkv_b, *([bias_tab] * WG_GROUPS))


def _sigmoid(x):
    return 1.0 / (1.0 + jnp.exp(-x))


def _layer_norm(x, g, b):
    mu = jnp.mean(x, axis=-1, keepdims=True)
    xc = x - mu
    var = jnp.mean(xc * xc, axis=-1, keepdims=True)
    return xc * lax.rsqrt(var + LN_EPS) * g + b


def _merge_kernel(x_ref, ya_ref, yb_ref, wg_ref, bg_ref, wa_ref, wb_ref, wo_ref,
                  g_ref, b_ref, wrh_ref, wrl_ref,
                  h8_ref, aff2_ref, aff3_ref, *, alpha, sub):
    tm, D = x_ref.shape
    for st in range(tm // sub):
        r0 = st * sub
        x = x_ref[r0:r0 + sub, :]
        gates = _dot(x.astype(BF16), wg_ref[...]) + bg_ref[...]
        a = _dot(ya_ref[r0:r0 + sub, :], wa_ref[...])
        bm = _dot(yb_ref[r0:r0 + sub, :], wb_ref[...])
        mix = _sigmoid(gates[:, :D]) * a + _sigmoid(gates[:, D:]) * bm
        o = _dot(mix.astype(BF16), wo_ref[...])
        h = _layer_norm(alpha * x + o, g_ref[...], b_ref[...])
        for j in range(D // LANES):
            h8_ref[pl.ds(r0 * 8 + j, sub, stride=8), :] = h[:, j * LANES:(j + 1) * LANES]
        h_hi = h.astype(BF16)
        h_lo = (h - h_hi.astype(F32)).astype(BF16)
        lg = _nt(wrh_ref[...], h_hi) + (_nt(wrl_ref[...], h_hi) + _nt(wrh_ref[...], h_lo))
        mx = jnp.max(lg, axis=0, keepdims=True)
        ex = jnp.exp(lg - mx)
        aff = ex / jnp.sum(ex, axis=0, keepdims=True)
        aff2_ref[0, :, r0:r0 + sub] = aff
        for c in range(sub // LANES):
            cc = r0 // LANES + c
            for e in range(N_EXPERTS):
                aff3_ref[0, e, cc:cc + 1, :] = aff[e:e + 1, c * LANES:(c + 1) * LANES]


def _merge(x2, ya, yb, wg, bg, wa, wb, wo, g1, b1, wr_hi, wr_lo, B, S, alpha, tm=1024, sub=1024):
    T, D = x2.shape
    nt = S // tm
    E = N_EXPERTS
    const = lambda i: (0, 0)
    return pl.pallas_call(
        functools.partial(_merge_kernel, alpha=alpha, sub=sub),
        grid=(T // tm,),
        in_specs=[
            pl.BlockSpec((tm, D), lambda i: (i, 0)),
            pl.BlockSpec((tm, ya.shape[1]), lambda i: (i, 0)),
            pl.BlockSpec((tm, yb.shape[1]), lambda i: (i, 0)),
            pl.BlockSpec(wg.shape, const),
            pl.BlockSpec(bg.shape, const),
            pl.BlockSpec(wa.shape, const),
            pl.BlockSpec(wb.shape, const),
            pl.BlockSpec(wo.shape, const),
            pl.BlockSpec(g1.shape, const),
            pl.BlockSpec(b1.shape, const),
            pl.BlockSpec(wr_hi.shape, const),
            pl.BlockSpec(wr_lo.shape, const),
        ],
        out_specs=[
            pl.BlockSpec((tm * 8, LANES), lambda i: (i, 0)),
            pl.BlockSpec((1, E, tm), lambda i: (i // nt, 0, i % nt)),
            pl.BlockSpec((1, E, tm // LANES, LANES), lambda i: (i // nt, 0, i % nt, 0)),
        ],
        out_shape=[
            jax.ShapeDtypeStruct((T * 8, LANES), F32),
            jax.ShapeDtypeStruct((B, E, S), F32),
            jax.ShapeDtypeStruct((B, E, S // LANES, LANES), F32),
        ],
        compiler_params=_cparams(("arbitrary",)),
        name="merge_ln_router",
    )(x2, ya, yb, wg, bg, wa, wb, wo, g1, b1, wr_hi, wr_lo)


def _onehot(cond):
    return jnp.where(cond, 1.0, 0.0).astype(BF16)


def _route_kernel(aff2_ref, aff3_ref, idx_ref, gate_ref, *, cap):
    E, nch = aff3_ref.shape[1], aff3_ref.shape[2]
    capf = float(cap)
    a2 = aff2_ref[0]

    def bit_step(it, prefix):
        cand = prefix | jnp.left_shift(jnp.int32(1), 30 - it)
        cnt = jnp.sum(jnp.where(a2 >= lax.bitcast_convert_type(cand, F32), 1.0, 0.0),
                      axis=1, keepdims=True)
        return jnp.where((cnt >= capf) & (cand >= MIN_NORMAL_BITS), cand, prefix)

    thr = lax.bitcast_convert_type(lax.fori_loop(0, 31, bit_step, jnp.zeros((E, 1), I32)), F32)
    n_gt = jnp.sum(jnp.where(a2 > thr, 1.0, 0.0), axis=1, keepdims=True)
    need = capf - n_gt

    ri = lax.broadcasted_iota(I32, (LANES, LANES), 0)
    ci = lax.broadcasted_iota(I32, (LANES, LANES), 1)
    upper = _onehot(ri <= ci)
    ones_l = jnp.ones((LANES, LANES), BF16)
    rc = lax.broadcasted_iota(I32, (nch, nch), 0)
    cc = lax.broadcasted_iota(I32, (nch, nch), 1)
    lower_strict = _onehot(cc < rc)
    upper_c = _onehot(rc <= cc)
    ones_c = jnp.ones((nch, nch), BF16)
    crow = lax.broadcasted_iota(I32, (nch, LANES), 0)
    s_b = lax.broadcasted_iota(I32, (cap, LANES), 0).astype(F32)
    s_c = lax.broadcasted_iota(I32, (cap, nch), 0).astype(F32)
    chunk_id = lax.broadcasted_iota(I32, (cap, nch), 1).astype(F32)
    lane_id = lax.broadcasted_iota(I32, (cap, LANES), 1).astype(F32)

    def prefix(mask_b):
        local = _dot(mask_b, upper)
        tot_b = _dot(mask_b, ones_l).astype(BF16)
        return local, _dot(lower_strict, tot_b)

    for e in range(E):
        xe = aff3_ref[0, e]
        te = thr[e:e + 1, :]
        gt = xe > te
        eq = xe == te
        eq_f = jnp.where(eq, 1.0, 0.0)
        loc_eq, off_eq = prefix(eq_f.astype(BF16))
        rank_excl = loc_eq + off_eq - eq_f
        sel = gt | (eq & (rank_excl < need[e:e + 1, :]))
        sel_b = _onehot(sel)
        loc = _dot(sel_b, upper)
        tot_r = _nt(jnp.ones((LANES, LANES), BF16), sel_b)
        tot_rb = tot_r.astype(BF16)
        ct_r = _dot(tot_rb, upper_c)
        g = _onehot(ct_r[0:1, :] <= s_c)
        off_b = _nt(g, tot_rb)
        j_b = _dot(g, ones_c)
        j_r = _nt(jnp.ones((8, nch), BF16), g)
        nxt = pltpu.roll(loc, nch - 1, 0)
        dif = jnp.where(crow < nch - 1, nxt - loc, 0.0).astype(BF16)
        lg = loc[0:1, :] + _dot(g, dif)
        ind = _onehot(lg <= s_b - off_b)
        loc_b = _dot(ind, ones_l)
        loc_r = _nt(jnp.ones((8, LANES), BF16), ind)
        idx_ref[0, e:e + 1, :] = (j_r[0:1, :] * float(LANES) + loc_r[0:1, :]).astype(I32)
        oh = _onehot(chunk_id == j_b)
        pick = lane_id == loc_b
        x_hi = xe.astype(BF16)
        r1 = xe - x_hi.astype(F32)
        x_mid = r1.astype(BF16)
        x_lo = (r1 - x_mid.astype(F32)).astype(BF16)
        ones8 = jnp.ones((8, LANES), BF16)
        gate = None
        for piece in (x_hi, x_mid, x_lo):
            rows = _dot(oh, piece)
            val = _nt(ones8, jnp.where(pick, rows, 0.0).astype(BF16))
            gate = val if gate is None else gate + val
        gate_ref[0, e:e + 1, :] = gate[0:1, :]


def _route(aff2, aff3, cap):
    B, E, S = aff2.shape
    nch = S // LANES
    return pl.pallas_call(
        functools.partial(_route_kernel, cap=cap),
        grid=(B,),
        in_specs=[
            pl.BlockSpec((1, E, S), lambda b: (b, 0, 0)),
            pl.BlockSpec((1, E, nch, LANES), lambda b: (b, 0, 0, 0)),
        ],
        out_specs=[
            pl.BlockSpec((1, E, cap), lambda b: (b, 0, 0)),
            pl.BlockSpec((1, E, cap), lambda b: (b, 0, 0)),
        ],
        out_shape=[
            jax.ShapeDtypeStruct((B, E, cap), I32),
            jax.ShapeDtypeStruct((B, E, cap), F32),
        ],
        compiler_params=_cparams(("arbitrary",)),
        name="route",
    )(aff2, aff3)


def _gather_kernel(idx_ref, h8_hbm, xe_ref, hbuf, xr, sem, *, cap, unroll):
    b = pl.program_id(0)
    e = pl.program_id(1)
    D = xe_ref.shape[3]

    @pl.when(e == 0)
    def _load_tokens():
        cp = pltpu.make_async_copy(h8_hbm.at[b], hbuf, sem)
        cp.start()
        cp.wait()

    def body(k, carry):
        base = k * unroll
        slabs = []
        for u in range(unroll):
            t = idx_ref[0, 0, base + u]
            slabs.append(hbuf[pl.ds(pl.multiple_of(t * 8, 8), 8), :])
        for u in range(unroll):
            xr[pl.ds(pl.multiple_of((base + u) * 8, 8), 8), :] = slabs[u]
        return carry

    lax.fori_loop(0, cap // unroll, body, 0)
    for j in range(D // LANES):
        xe_ref[0, 0, :, j * LANES:(j + 1) * LANES] = xr[pl.ds(j, cap, stride=8), :].astype(xe_ref.dtype)


def _gather_tokens(idx3, h8, B, S, D, cap, unroll=16):
    E = N_EXPERTS
    return pl.pallas_call(
        functools.partial(_gather_kernel, cap=cap, unroll=unroll),
        grid=(B, E),
        in_specs=[
            pl.BlockSpec((1, 1, cap), lambda b, e: (b * E + e, 0, 0),
                         memory_space=pltpu.MemorySpace.SMEM),
            pl.BlockSpec(memory_space=pl.ANY),
        ],
        out_specs=pl.BlockSpec((1, 1, cap, D), lambda b, e: (b, e, 0, 0)),
        out_shape=jax.ShapeDtypeStruct((B, E, cap, D), BF16),
        scratch_shapes=[
            pltpu.VMEM((S * 8, LANES), F32),
            pltpu.VMEM((cap * 8, LANES), F32),
            pltpu.SemaphoreType.DMA(()),
        ],
        compiler_params=_cparams(("arbitrary", "arbitrary")),
        name="gather_tokens",
    )(idx3, h8)


def _ffn_kernel(xe_ref, wg_ref, wu_ref, wd_ref, y_ref, wgs, wus, wds, *, sub, wrows):
    b = pl.program_id(1)
    f = pl.program_id(2)
    cap, D = xe_ref.shape[2], xe_ref.shape[3]
    fb = wg_ref.shape[2]

    @pl.when(b == 0)
    def _cast_weights():
        for r in range(D // wrows):
            wgs[f, r * wrows:(r + 1) * wrows, :] = wg_ref[0, r * wrows:(r + 1) * wrows, :].astype(BF16)
            wus[f, r * wrows:(r + 1) * wrows, :] = wu_ref[0, r * wrows:(r + 1) * wrows, :].astype(BF16)
        for r in range(fb // wrows):
            wds[f, r * wrows:(r + 1) * wrows, :] = wd_ref[0, r * wrows:(r + 1) * wrows, :].astype(BF16)

    def compute(accumulate):
        for st in range(cap // sub):
            xs = xe_ref[0, 0, st * sub:(st + 1) * sub, :]
            g = _dot(xs, wgs[f])
            u = _dot(xs, wus[f])
            hid = (g * _sigmoid(g)) * u
            y = _dot(hid.astype(BF16), wds[f])
            for j in range(D // LANES):
                rows = pl.ds(st * sub * 8 + j, sub, stride=8)
                yj = y[:, j * LANES:(j + 1) * LANES]
                y_ref[0, 0, rows, :] = y_ref[0, 0, rows, :] + yj if accumulate else yj

    pl.when(f == 0)(functools.partial(compute, False))
    pl.when(f > 0)(functools.partial(compute, True))


def _ffn(xe, wg, wu, wd, nf=2, sub=256, wrows=256):
    B, E, cap, D = xe.shape
    FF = wg.shape[2]
    fb = FF // nf

    def wchunk(b, f):
        return jnp.where(b == 0, f, nf - 1)

    return pl.pallas_call(
        functools.partial(_ffn_kernel, sub=sub, wrows=wrows),
        grid=(E, B, nf),
        in_specs=[
            pl.BlockSpec((1, 1, cap, D), lambda e, b, f: (b, e, 0, 0)),
            pl.BlockSpec((1, D, fb), lambda e, b, f: (e, 0, wchunk(b, f))),
            pl.BlockSpec((1, D, fb), lambda e, b, f: (e, 0, wchunk(b, f))),
            pl.BlockSpec((1, fb, D), lambda e, b, f: (e, wchunk(b, f), 0)),
        ],
        out_specs=pl.BlockSpec((1, 1, cap * 8, LANES), lambda e, b, f: (b, e, 0, 0)),
        out_shape=jax.ShapeDtypeStruct((B, E, cap * 8, LANES), F32),
        scratch_shapes=[
            pltpu.VMEM((nf, D, fb), BF16),
            pltpu.VMEM((nf, D, fb), BF16),
            pltpu.VMEM((nf, fb, D), BF16),
        ],
        compiler_params=_cparams(("arbitrary", "arbitrary", "arbitrary")),
        name="expert_ffn",
    )(xe, wg, wu, wd)


def _combine_kernel(idx_ref, gate_ref, y_ref, h8_hbm, g_ref, b_ref, o_ref, acc, sem,
                    *, alpha, ne, cap, unroll):
    b = pl.program_id(0)
    j = pl.program_id(1)
    sub, D = o_ref.shape[1], o_ref.shape[2]
    rows_blk = 2048

    @pl.when(j == 0)
    def _init():
        cp = pltpu.make_async_copy(h8_hbm.at[b], acc, sem)
        cp.start()
        cp.wait()

        def body(k, carry):
            r = pl.multiple_of(k * rows_blk, rows_blk)
            acc[pl.ds(r, rows_blk), :] = alpha * acc[pl.ds(r, rows_blk), :]
            return carry

        lax.fori_loop(0, acc.shape[0] // rows_blk, body, 0)

    @pl.when(j < ne)
    def _scatter():
        def body(k, carry):
            base = k * unroll
            dsts, vals = [], []
            for u in range(unroll):
                i = base + u
                dst = pl.multiple_of(idx_ref[0, 0, i] * 8, 8)
                src = pl.multiple_of(i * 8, 8)
                dsts.append(dst)
                vals.append(acc[pl.ds(dst, 8), :] + gate_ref[0, 0, i] * y_ref[0, 0, pl.ds(src, 8), :])
            for u in range(unroll):
                acc[pl.ds(dsts[u], 8), :] = vals[u]
            return carry

        lax.fori_loop(0, cap // unroll, body, 0)

    @pl.when(j >= ne)
    def _finish():
        base = pl.multiple_of((j - ne) * (sub * 8), sub * 8)
        blk = acc.at[pl.ds(base, sub * 8)]
        parts = [blk[pl.ds(c, sub, stride=8), :] for c in range(D // LANES)]
        x = jnp.concatenate(parts, axis=1)
        o_ref[0] = _layer_norm(x, g_ref[...], b_ref[...])


def _combine(idx3, gate3, y, h8, g2, b2, B, S, D, cap, alpha, sub=512, unroll=8):
    E = N_EXPERTS
    nst = S // sub
    last = E - 1
    return pl.pallas_call(
        functools.partial(_combine_kernel, alpha=alpha, ne=E, cap=cap, unroll=unroll),
        grid=(B, E + nst),
        in_specs=[
            pl.BlockSpec((1, 1, cap), lambda b, j: (b * E + jnp.minimum(j, last), 0, 0),
                         memory_space=pltpu.MemorySpace.SMEM),
            pl.BlockSpec((1, 1, cap), lambda b, j: (b * E + jnp.minimum(j, last), 0, 0),
                         memory_space=pltpu.MemorySpace.SMEM),
            pl.BlockSpec((1, 1, cap * 8, LANES), lambda b, j: (b, jnp.minimum(j, last), 0, 0)),
            pl.BlockSpec(memory_space=pl.ANY),
            pl.BlockSpec((1, D), lambda b, j: (0, 0)),
            pl.BlockSpec((1, D), lambda b, j: (0, 0)),
        ],
        out_specs=pl.BlockSpec((1, sub, D), lambda b, j: (b, jnp.maximum(j - E, 0), 0)),
        out_shape=jax.ShapeDtypeStruct((B, S, D), F32),
        scratch_shapes=[pltpu.VMEM((S * 8, LANES), F32), pltpu.SemaphoreType.DMA(())],
        compiler_params=_cparams(("arbitrary", "arbitrary")),
        name="combine_ln",
    )(idx3, gate3, y, h8, g2, b2)


def _layer(x, w_in, b_in, rpb, sink, w_branch_a, w_branch_b, w_out, ln1_g, ln1_b,
           w_router, w_gate, w_up, w_down, ln2_g, ln2_b, alpha):
    B, S, D = x.shape
    T = B * S
    na_w = NA_HEADS * HEAD_DIM
    wq_w = WG_HEADS * HEAD_DIM
    kv_w = WG_KV_HEADS * HEAD_DIM
    o = np.cumsum([0, na_w, na_w, na_w, wq_w, kv_w, kv_w, D, D])
    scale = HEAD_DIM ** -0.5
    perm = np.concatenate([np.arange(h * HEAD_DIM, (h + 1) * HEAD_DIM) for h in WG_HEAD_ORDER])

    def cols(a):
        qa = a[..., o[0]:o[1]] * scale
        qb = (a[..., o[3]:o[4]] * scale)[..., perm]
        return jnp.concatenate([qa, a[..., o[1]:o[3]], qb, a[..., o[4]:o[6]]], axis=-1)

    w_qkv = cols(w_in).astype(BF16)
    b_qkv = cols(b_in)[None, :]
    w_g = w_in[:, o[6]:].astype(BF16)
    b_g = b_in[None, o[6]:]

    x2 = x.reshape(T, D)
    qkv_a, qkv_b = _qkv_proj(x2, w_qkv, b_qkv, 3 * na_w)
    ya = _natten(qkv_a.reshape(B, S, 3 * na_w), _na_bias_tables(rpb, S // GRID_W), B, S)
    yb = _wgqa(qkv_b.reshape(B, S, wq_w + 2 * kv_w), _wg_bias_tables(S), sink, B, S)

    wr_t = w_router.T
    wr_hi = wr_t.astype(BF16)
    wr_lo = (wr_t - wr_hi.astype(F32)).astype(BF16)
    h8, aff2, aff3 = _merge(
        x2, ya.reshape(T, na_w), yb.reshape(T, wq_w), w_g, b_g,
        w_branch_a.astype(BF16), w_branch_b[perm].astype(BF16), w_out.astype(BF16),
        ln1_g[None, :], ln1_b[None, :], wr_hi, wr_lo, B, S, alpha)

    cap = EC_CAPACITY_FACTOR * S // N_EXPERTS
    idx, gate = _route(aff2, aff3, cap)
    idx3 = idx.reshape(B * N_EXPERTS, 1, cap)
    gate3 = gate.reshape(B * N_EXPERTS, 1, cap)
    xe = _gather_tokens(idx3, h8.reshape(B, S * 8, LANES), B, S, D, cap)
    y = _ffn(xe, w_gate, w_up, w_down)
    return _combine(idx3, gate3, y, h8.reshape(B, S * 8, LANES),
                    ln2_g[None, :], ln2_b[None, :], B, S, D, cap, alpha)


def kernel(x, w_in, b_in, rpb, sink, w_branch_a, w_branch_b, w_out, ln1_g, ln1_b,
           w_router, w_gate, w_up, w_down, ln2_g, ln2_b):
    depth = w_in.shape[0]
    alpha = (2 * depth) ** 0.25
    for l in range(depth):
        x = _layer(x, w_in[l], b_in[l], rpb[l], sink[l], w_branch_a[l], w_branch_b[l],
                   w_out[l], ln1_g[l], ln1_b[l], w_router[l], w_gate[l], w_up[l],
                   w_down[l], ln2_g[l], ln2_b[l], alpha)
    return x
```
